```python
import jax, jax.numpy as jnp
from jax import lax
import numpy as np

D_MODEL = 1024
BATCH = 8
SEQ = 4096
DEPTH = 2
DEC_BATCH = 4
DEC_SEQ = 8192
PAST_LEN = 128

GRID_W = 64
NA_HEADS = 8
NA_HD = 64
NA_WIDTH = NA_HEADS * NA_HD
WIN_R = 8
WIN_C = 16
HG_HEADS = 4
HG_DK = 128
HG_DV = 128
HG_KW = HG_HEADS * HG_DK
HG_VW = HG_HEADS * HG_DV
HG_CHUNK = 64
F_MIN = 1e-20
D_CONV = 512
CONV_K = 31
N_BRANCH = 3
MEM_TOKENS = 256
MEM_HEADS = 4
MEM_HD = 128
MEM_WIDTH = MEM_HEADS * MEM_HD
N_EXPERTS = 16
EC_CAPACITY = 2
D_FF = 2752
EPS = 1e-6

_IN_SIZES = (NA_WIDTH, NA_WIDTH, NA_WIDTH, HG_KW, HG_KW, HG_KW, HG_VW, HG_VW,
             D_CONV, D_CONV, D_MODEL, D_MODEL, D_MODEL)
D_IN = 3 * NA_WIDTH + 3 * HG_KW + 2 * HG_VW + 2 * D_CONV + N_BRANCH * D_MODEL

kernel_name = "hybrid_natten_hgrn2_conformer_ec_encoder"

F32 = jnp.float32


def _split_points():
    pts, acc = [], 0
    for s in _IN_SIZES[:-1]:
        acc += s
        pts.append(acc)
    return pts


def _rms(x, g):
    xf = x.astype(F32)
    y = xf * lax.rsqrt(jnp.mean(xf * xf, axis=-1, keepdims=True) + EPS)
    return (y * g.astype(F32)).astype(x.dtype)


def _neighbourhood_attention(q, k, v, rel_bias):
    B, L, H, hd = q.shape
    rows = L // GRID_W
    wr = min(WIN_R, rows)
    qg = q.reshape(B, rows, GRID_W, H, hd)
    kg = k.reshape(B, rows, GRID_W, H, hd)
    vg = v.reshape(B, rows, GRID_W, H, hd)
    col = jnp.arange(GRID_W)
    c_idx = jnp.clip(col - WIN_C // 2, 0, GRID_W - WIN_C)[:, None] + jnp.arange(WIN_C)[None, :]
    dc_idx = (c_idx - col[:, None] + WIN_C - 1)[:, None, :]
    scale = hd ** -0.5

    def row_block(r):
        r0 = jnp.clip(r - wr // 2, 0, rows - wr)
        k_nb = lax.dynamic_slice_in_dim(kg, r0, wr, axis=1)[:, :, c_idx]
        v_nb = lax.dynamic_slice_in_dim(vg, r0, wr, axis=1)[:, :, c_idx]
        q_r = lax.dynamic_index_in_dim(qg, r, axis=1, keepdims=False)
        dr_idx = (r0 + jnp.arange(wr) - r + WIN_R - 1)[None, :, None]
        bias = rel_bias[:, dr_idx, dc_idx].astype(F32)
        s = jnp.einsum('bqhd,brqchd->bhqrc', q_r, k_nb).astype(F32) * scale + bias[None]
        p = jax.nn.softmax(s.reshape(B, H, GRID_W, wr * WIN_C), axis=-1).reshape(s.shape)
        return jnp.einsum('bhqrc,brqchd->bqhd', p.astype(v.dtype), v_nb)

    o = lax.map(row_block, jnp.arange(rows))
    return o.transpose(1, 0, 2, 3, 4).reshape(B, L, H * hd)


def _hgrn2_direction(q, k, logf, v):
    B, L, H, dk = q.shape
    dv = v.shape[-1]
    n = L // HG_CHUNK

    def to_chunks(a):
        return a.reshape(B, n, HG_CHUNK, H, a.shape[-1]).transpose(1, 0, 3, 2, 4)

    lower = jnp.tril(jnp.ones((HG_CHUNK, HG_CHUNK), dtype=bool))[:, :, None]

    def step(S, blk):
        qi, ki, gi, vi = blk
        A = jnp.cumsum(gi, axis=2)
        o_inter = jnp.einsum('bhtk,bhkv->bhtv', qi * jnp.exp(A), S)
        diff = A[:, :, :, None, :] - A[:, :, None, :, :]
        decay = jnp.where(lower, jnp.exp(jnp.where(lower, diff, 0.0)), 0.0)
        att = jnp.einsum('bhtk,bhtsk,bhsk->bhts', qi, decay, ki)
        o_intra = jnp.einsum('bhts,bhsv->bhtv', att, vi)
        A_end = A[:, :, -1, :]
        S = jnp.exp(A_end)[..., None] * S + jnp.einsum(
            'bhsk,bhsv->bhkv', ki * jnp.exp(A_end[:, :, None, :] - A), vi)
        return S, o_inter + o_intra

    S0 = jnp.zeros((B, H, dk, dv), F32)
    _, o = lax.scan(step, S0, tuple(map(to_chunks, (q, k, logf, v))))
    return o.transpose(1, 0, 3, 2, 4).reshape(B, L, H, dv)


def _hgrn2_branch(hq, hff, hfb, hv, hg, lb_f, lb_b, out_g):
    B, L, _ = hq.shape

    def heads(a, d):
        return a.astype(F32).reshape(B, L, HG_HEADS, d)

    q = heads(jax.nn.silu(hq.astype(F32)) * HG_DK ** -0.5, HG_DK)
    v = heads(hv, HG_DV)

    def gate(hf, lb):
        hf = hf.astype(F32)
        f = lb + (1.0 - lb) * jax.nn.sigmoid(hf)
        logf = jnp.log(jnp.maximum(f, F_MIN))
        k = (1.0 - lb) * jax.nn.sigmoid(-hf)
        return heads(logf, HG_DK), heads(k, HG_DK)

    logf_f, k_f = gate(hff, lb_f)
    logf_b, k_b = gate(hfb, lb_b)
    o_f = _hgrn2_direction(q, k_f, logf_f, v)
    flip = lambda a: jnp.flip(a, axis=1)
    o_b = flip(_hgrn2_direction(flip(q), flip(k_b), flip(logf_b), flip(v)))
    o = _rms(o_f + o_b, out_g) * jax.nn.silu(heads(hg, HG_DV))
    return o.reshape(B, L, HG_VW).astype(hq.dtype)


def _conv_module(a, b, conv_w, conv_b, ln_g, ln_b):
    u = a * jax.nn.sigmoid(b)
    y = lax.conv_general_dilated(
        u, conv_w[:, None, :].astype(u.dtype), window_strides=(1,),
        padding=[(CONV_K // 2, CONV_K // 2)], dimension_numbers=('NWC', 'WIO', 'NWC'),
        feature_group_count=D_CONV)
    yf = y.astype(F32) + conv_b.astype(F32)
    mu = jnp.mean(yf, axis=-1, keepdims=True)
    var = jnp.mean(jnp.square(yf - mu), axis=-1, keepdims=True)
    yn = (yf - mu) * lax.rsqrt(var + EPS) * ln_g.astype(F32) + ln_b.astype(F32)
    return jax.nn.silu(yn).astype(a.dtype)


def _memory_attention(h, mem_n, wq, wkv, qn, kn, wo):
    B, L, _ = h.shape
    M = mem_n.shape[1]
    q = _rms((h @ wq).reshape(B, L, MEM_HEADS, MEM_HD), qn)
    k, v = jnp.split(mem_n @ wkv, 2, axis=-1)
    k = _rms(k.reshape(B, M, MEM_HEADS, MEM_HD), kn)
    v = v.reshape(B, M, MEM_HEADS, MEM_HD)
    s = jnp.einsum('blhd,bmhd->bhlm', q, k).astype(F32) * MEM_HD ** -0.5
    p = jax.nn.softmax(s, axis=-1)
    o = jnp.einsum('bhlm,bmhd->blhd', p.astype(v.dtype), v).reshape(B, L, MEM_WIDTH)
    return o @ wo


def _expert_choice_ffn(h, w_router, w_gu, w_down):
    B, L, D = h.shape
    N = B * L
    cap = EC_CAPACITY * N // N_EXPERTS
    hf = h.reshape(N, D)
    aff = jax.nn.softmax((hf @ w_router).astype(F32), axis=-1)
    g, idx = lax.top_k(aff.T, cap)
    xe = hf[idx]

    def expert(args):
        xi, wgu, wd = args
        gate, up = jnp.split(xi @ wgu, 2, axis=-1)
        return (jax.nn.silu(gate) * up) @ wd

    ye = lax.map(expert, (xe, w_gu, w_down)) * g[..., None].astype(h.dtype)
    out = jnp.zeros((N, D), ye.dtype).at[idx.reshape(-1)].add(ye.reshape(-1, D))
    return out.reshape(B, L, D)


def _trunk(x, mem, norm_mix, w_in, na_q_norm, na_k_norm, na_rel_bias, w_na_br, hg_lb, hg_out_norm,
           w_hg_br, conv_w, conv_b, conv_ln_g, conv_ln_b, w_cv_br, w_out, norm_mem, mem_kv_norm,
           wq_mem, wkv_mem, mem_q_norm, mem_k_norm, wo_mem, norm_ffn, w_router, w_gate_up, w_down):
    B, L, _ = x.shape
    P = jax.nn.softmax(hg_lb.astype(F32), axis=1)
    lb = jnp.cumsum(P, axis=1) - P[:, :1]
    pts = _split_points()
    for l in range(DEPTH):
        h = _rms(x, norm_mix[l])
        (na_q, na_k, na_v, hg_q, hg_ff, hg_fb, hg_v, hg_g, cv_a, cv_b,
         g_na, g_hg, g_cv) = jnp.split(h @ w_in[l], pts, axis=-1)
        q = _rms(na_q.reshape(B, L, NA_HEADS, NA_HD), na_q_norm[l])
        k = _rms(na_k.reshape(B, L, NA_HEADS, NA_HD), na_k_norm[l])
        v = na_v.reshape(B, L, NA_HEADS, NA_HD)
        br_na = _neighbourhood_attention(q, k, v, na_rel_bias[l]) @ w_na_br[l]
        br_hg = _hgrn2_branch(hg_q, hg_ff, hg_fb, hg_v, hg_g, lb[0, l], lb[1, l], hg_out_norm[l]) @ w_hg_br[l]
        br_cv = _conv_module(cv_a, cv_b, conv_w[l], conv_b[l], conv_ln_g[l], conv_ln_b[l]) @ w_cv_br[l]
        merged = (jax.nn.sigmoid(g_na) * br_na + jax.nn.sigmoid(g_hg) * br_hg
                  + jax.nn.sigmoid(g_cv) * br_cv)
        x = x + merged @ w_out[l]
        x = x + _memory_attention(_rms(x, norm_mem[l]), _rms(mem, mem_kv_norm[l]), wq_mem[l], wkv_mem[l],
                                  mem_q_norm[l], mem_k_norm[l], wo_mem[l])
        x = x + _expert_choice_ffn(_rms(x, norm_ffn[l]), w_router[l], w_gate_up[l], w_down[l])
    return x


def setup_inputs(seed: int = 0) -> dict:
    key = jax.random.key(seed)
    ks = jax.random.split(key, 32)
    nrm = lambda k, shape, s: jax.random.normal(k, shape, F32) * s
    gain = lambda k, shape: 1.0 + 0.1 * jax.random.normal(k, shape, F32)
    return {
        "x_prompt": nrm(ks[0], (BATCH, SEQ, D_MODEL), 1.0),
        "x_sample": nrm(ks[1], (DEC_BATCH, DEC_SEQ, D_MODEL), 1.0),
        "mem_prompt": nrm(ks[2], (BATCH, MEM_TOKENS, D_MODEL), 1.0),
        "mem_sample": nrm(ks[3], (DEC_BATCH, MEM_TOKENS, D_MODEL), 1.0),
        "norm_mix": gain(ks[4], (DEPTH, D_MODEL)),
        "w_in": nrm(ks[5], (DEPTH, D_MODEL, D_IN), D_MODEL ** -0.5),
        "na_q_norm": gain(ks[6], (DEPTH, NA_HD)),
        "na_k_norm": gain(ks[7], (DEPTH, NA_HD)),
        "na_rel_bias": nrm(ks[8], (DEPTH, NA_HEADS, 2 * WIN_R - 1, 2 * WIN_C - 1), 0.1),
        "w_na_br": nrm(ks[9], (DEPTH, NA_WIDTH, D_MODEL), NA_WIDTH ** -0.5),
        "hg_lb": nrm(ks[10], (2, DEPTH, HG_KW), 0.5),
        "hg_out_norm": gain(ks[11], (DEPTH, HG_DV)),
        "w_hg_br": nrm(ks[12], (DEPTH, HG_VW, D_MODEL), HG_VW ** -0.5),
        "conv_w": nrm(ks[13], (DEPTH, CONV_K, D_CONV), CONV_K ** -0.5),
        "conv_b": nrm(ks[14], (DEPTH, D_CONV), 0.01),
        "conv_ln_g": gain(ks[15], (DEPTH, D_CONV)),
        "conv_ln_b": nrm(ks[16], (DEPTH, D_CONV), 0.01),
        "w_cv_br": nrm(ks[17], (DEPTH, D_CONV, D_MODEL), D_CONV ** -0.5),
        "w_out": nrm(ks[18], (DEPTH, D_MODEL, D_MODEL), D_MODEL ** -0.5),
        "norm_mem": gain(ks[19], (DEPTH, D_MODEL)),
        "mem_kv_norm": gain(ks[20], (DEPTH, D_MODEL)),
        "wq_mem": nrm(ks[21], (DEPTH, D_MODEL, MEM_WIDTH), D_MODEL ** -0.5),
        "wkv_mem": nrm(ks[22], (DEPTH, D_MODEL, 2 * MEM_WIDTH), D_MODEL ** -0.5),
        "mem_q_norm": gain(ks[23], (DEPTH, MEM_HD)),
        "mem_k_norm": gain(ks[24], (DEPTH, MEM_HD)),
        "wo_mem": nrm(ks[25], (DEPTH, MEM_WIDTH, D_MODEL), MEM_WIDTH ** -0.5),
        "norm_ffn": gain(ks[26], (DEPTH, D_MODEL)),
        "w_router": nrm(ks[27], (DEPTH, D_MODEL, N_EXPERTS), D_MODEL ** -0.5),
        "w_gate_up": nrm(ks[28], (DEPTH, N_EXPERTS, D_MODEL, 2 * D_FF), D_MODEL ** -0.5),
        "w_down": nrm(ks[29], (DEPTH, N_EXPERTS, D_FF, D_MODEL), D_FF ** -0.5),
    }


def reference(x_prompt, x_sample, mem_prompt, mem_sample, norm_mix, w_in, na_q_norm, na_k_norm,
              na_rel_bias, w_na_br, hg_lb, hg_out_norm, w_hg_br, conv_w, conv_b, conv_ln_g, conv_ln_b,
              w_cv_br, w_out, norm_mem, mem_kv_norm, wq_mem, wkv_mem, mem_q_norm, mem_k_norm, wo_mem,
              norm_ffn, w_router, w_gate_up, w_down):
    params = (norm_mix, w_in, na_q_norm, na_k_norm, na_rel_bias, w_na_br, hg_lb, hg_out_norm, w_hg_br,
              conv_w, conv_b, conv_ln_g, conv_ln_b, w_cv_br, w_out, norm_mem, mem_kv_norm, wq_mem,
              wkv_mem, mem_q_norm, mem_k_norm, wo_mem, norm_ffn, w_router, w_gate_up, w_down)
    y_prompt = _trunk(x_prompt, mem_prompt, *params)
    y_sample = _trunk(x_sample, mem_sample, *params)
    return (y_prompt, y_sample)
```

```python
import functools

import jax
import jax.numpy as jnp
from jax import lax
from jax.experimental import pallas as pl
from jax.experimental.pallas import tpu as pltpu

F32 = jnp.float32
BF16 = jnp.bfloat16
I32 = jnp.int32

D_MODEL = 1024
GRID_W = 64
NA_HEADS = 8
NA_HD = 64
NA_WIDTH = NA_HEADS * NA_HD
WIN_R = 8
WIN_C = 16
HG_HEADS = 4
HG_DK = 128
HG_DV = 128
HG_KW = HG_HEADS * HG_DK
F_MIN = 1e-20
D_CONV = 512
CONV_K = 31
MEM_HEADS = 4
MEM_HD = 128
MEM_WIDTH = MEM_HEADS * MEM_HD
N_EXPERTS = 16
EC_CAPACITY = 2
D_FF = 2752
EPS = 1e-6
D_IN = 8192

COL_NA_Q, COL_NA_K, COL_NA_V = 0, 1, 2
COL_HG_Q, COL_HG_FF, COL_HG_FB, COL_HG_V, COL_HG_G = 3, 4, 5, 6, 7
COL_CV_A, COL_CV_B = 8, 9
COL_G_NA, COL_G_HG, COL_G_CV = 5, 6, 7

LANES = 128
D_FF_PAD = 2816
NEG_BIG = -1e30

HG_CHUNK = 256
SLOT_BLK = 256
TOK_BLK = 256
SUPER = 2048


def _cparams(sem, vmem_mb):
    return pltpu.CompilerParams(dimension_semantics=sem, vmem_limit_bytes=vmem_mb << 20)


def _sigmoid_pair(x):
    e = jnp.exp(-jnp.abs(x))
    r = 1.0 / (1.0 + e)
    er = e * r
    pos = x >= 0
    return jnp.where(pos, r, er), jnp.where(pos, er, r)


def _sigmoid(x):
    return _sigmoid_pair(x)[0]


def _split2(a):
    hi = a.astype(BF16)
    lo = (a - hi.astype(F32)).astype(BF16)
    return hi, lo


def _dot(a, b):
    return jnp.dot(a, b, preferred_element_type=F32)


def _dot_nt(a, b):
    return lax.dot_general(a, b, (((1,), (1,)), ((), ())), preferred_element_type=F32)


def _dot_tn(a, b):
    return lax.dot_general(a, b, (((0,), (0,)), ((), ())), preferred_element_type=F32)


def _rms_rows(x, g):
    return x * lax.rsqrt(jnp.mean(x * x, axis=-1, keepdims=True) + EPS) * g


def _inproj_body(x_ref, g_ref, w_ref, grp_ref, qkg_ref, o_ref, h_ref):
    j = pl.program_id(1)

    @pl.when(j == 0)
    def _():
        h_ref[...] = _rms_rows(x_ref[...], g_ref[...]).astype(BF16)

    acc = _dot(h_ref[...], w_ref[...])

    @pl.when(j == 0)
    def _():
        hi, lo = _split2(acc * acc)
        ss = _dot(hi, grp_ref[...]) + _dot(lo, grp_ref[...])
        o_ref[...] = (acc * lax.rsqrt(ss * (1.0 / NA_HD) + EPS) * qkg_ref[...]).astype(BF16)

    @pl.when(j != 0)
    def _():
        o_ref[...] = acc.astype(BF16)


def _inproj(x2, g, w_bf, grp, qkg, tm=1024, tn=1024):
    n = x2.shape[0]
    return pl.pallas_call(
        _inproj_body,
        grid=(n // tm, D_IN // tn),
        in_specs=[
            pl.BlockSpec((tm, D_MODEL), lambda i, j: (i, 0)),
            pl.BlockSpec((1, D_MODEL), lambda i, j: (0, 0)),
            pl.BlockSpec((D_MODEL, tn), lambda i, j: (0, j)),
            pl.BlockSpec((tn, tn), lambda i, j: (0, 0)),
            pl.BlockSpec((1, tn), lambda i, j: (0, 0)),
        ],
        out_specs=pl.BlockSpec((tm, tn), lambda i, j: (i, j)),
        out_shape=jax.ShapeDtypeStruct((n, D_IN), BF16),
        scratch_shapes=[pltpu.VMEM((tm, D_MODEL), BF16)],
        compiler_params=_cparams(("parallel", "arbitrary"), 48),
    )(x2, g, w_bf, grp, qkg)


def _na_body(*refs):
    q_ref = refs[0]
    k_refs = refs[1:1 + WIN_R]
    v_refs = refs[1 + WIN_R:1 + 2 * WIN_R]
    bias_ref = refs[1 + 2 * WIN_R]
    o_ref = refs[2 + 2 * WIN_R]
    q = q_ref[...]
    k = jnp.concatenate([r[...] for r in k_refs], axis=0)
    v = jnp.concatenate([r[...] for r in v_refs], axis=0)
    outs = []
    for h in range(NA_HEADS):
        sl = slice(h * NA_HD, (h + 1) * NA_HD)
        s = _dot_nt(q[:, sl], k[:, sl]) + bias_ref[h]
        m = jnp.max(s, axis=-1, keepdims=True)
        p = jnp.exp(s - m)
        l = jnp.sum(p, axis=-1, keepdims=True)
        outs.append(_dot(p.astype(BF16), v[:, sl]) / l)
    o_ref[...] = jnp.concatenate(outs, axis=1).astype(BF16)


def _na(proj3, bias_tbl):
    b, l, _ = proj3.shape
    rows = l // GRID_W
    assert rows >= WIN_R

    def r0(r):
        return jnp.clip(r - WIN_R // 2, 0, rows - WIN_R)

    blk = (None, GRID_W, NA_WIDTH)
    in_specs = [pl.BlockSpec(blk, lambda bi, r: (bi, r, COL_NA_Q))]
    for col in (COL_NA_K, COL_NA_V):
        for i in range(WIN_R):
            in_specs.append(pl.BlockSpec(blk, lambda bi, r, i=i, col=col: (bi, r0(r) + i, col)))
    in_specs.append(pl.BlockSpec((None, NA_HEADS, GRID_W, WIN_R * GRID_W), lambda bi, r: (r - r0(r), 0, 0, 0)))
    return pl.pallas_call(
        _na_body,
        grid=(b, rows),
        in_specs=in_specs,
        out_specs=pl.BlockSpec(blk, lambda bi, r: (bi, r, 0)),
        out_shape=jax.ShapeDtypeStruct((b, l, NA_WIDTH), BF16),
        compiler_params=_cparams(("parallel", "arbitrary"), 32),
    )(*([proj3] * (1 + 2 * WIN_R)), bias_tbl)


def _na_bias_table(rel_bias):
    i = jnp.arange(WIN_R)
    var = jnp.arange(WIN_R)
    dr = i[None, :] - var[:, None] + WIN_R - 1
    c = jnp.arange(GRID_W)
    kc = jnp.arange(GRID_W)
    c0 = jnp.clip(c - WIN_C // 2, 0, GRID_W - WIN_C)
    valid = (kc[None, :] >= c0[:, None]) & (kc[None, :] < c0[:, None] + WIN_C)
    dc = jnp.clip(kc[None, :] - c[:, None] + WIN_C - 1, 0, 2 * WIN_C - 2)
    t = rel_bias.astype(F32)[:, dr[:, :, None, None], dc[None, None, :, :]]
    t = jnp.where(valid[None, None, None], t, NEG_BIG)
    t = t.transpose(1, 0, 3, 2, 4)
    return t.reshape(WIN_R, NA_HEADS, GRID_W, WIN_R * GRID_W)


def _hgrn_chunk(hq, hf, v, lb, st_ref, rev):
    c = hq.shape[0]
    hq = hq.astype(F32)
    q = hq * _sigmoid(hq) * (HG_DK ** -0.5)
    s_pos, s_neg = _sigmoid_pair(hf.astype(F32))
    f = lb + (1.0 - lb) * s_pos
    logf = jnp.log(jnp.maximum(f, F_MIN))
    k = (1.0 - lb) * s_neg

    t_col = lax.broadcasted_iota(I32, (c, 1), 0)
    t_mat = lax.broadcasted_iota(I32, (c, c), 0)
    s_mat = lax.broadcasted_iota(I32, (c, c), 1)
    tri = (s_mat >= t_mat) if rev else (s_mat <= t_mat)
    hi, lo = _split2(logf)
    cs = _dot(tri.astype(BF16), jnp.concatenate([hi, lo], axis=1))
    a = cs[:, :HG_DK] + cs[:, HG_DK:]

    st = st_ref[...]
    o = _dot_nt((q * jnp.exp(a)).astype(BF16), st.astype(BF16))

    att = jnp.where(t_mat == s_mat, jnp.sum(q * k, axis=1, keepdims=True), 0.0)
    a8 = a.reshape(c // 8, 8, HG_DK)
    sub = lax.broadcasted_iota(I32, (1, 8, 1), 1)
    m = c // 2
    while m >= 1:
        if m >= 4:
            ar = a.reshape(c // (2 * m), 2 * m, HG_DK)
            row = m if rev else m - 1
            ab = jnp.broadcast_to(ar[:, row:row + 1, :], ar.shape).reshape(c, HG_DK)
        else:
            off = m if rev else m - 1
            rows = [a8[:, p + off:p + off + 1, :] for p in range(0, 8, 2 * m)]
            ab = rows[-1]
            for idx in range(len(rows) - 2, -1, -1):
                ab = jnp.where(sub < (idx + 1) * 2 * m, rows[idx], ab)
            ab = jnp.broadcast_to(ab, a8.shape).reshape(c, HG_DK)
        e = jnp.exp(-jnp.abs(a - ab))
        upper = ((t_col // m) % 2) == 1
        q_half = jnp.logical_not(upper) if rev else upper
        qm = jnp.where(q_half, q * e, 0.0).astype(BF16)
        km = jnp.where(q_half, 0.0, k * e).astype(BF16)
        sc = _dot_nt(qm, km)
        att = att + jnp.where((t_mat // (2 * m)) == (s_mat // (2 * m)), sc, 0.0)
        m //= 2

    v_bf = v
    o = o + _dot(att.astype(BF16), v_bf)
    a_end = a[0:1, :] if rev else a[c - 1:c, :]
    k_end = (k * jnp.exp(a_end - a)).astype(BF16)
    st_ref[...] = st * jnp.exp(a_end) + _dot_tn(v_bf, k_end)
    return o


def _hgrn_body(qf_ref, ff_ref, vf_ref, qb_ref, fb_ref, vb_ref, lb_ref, of_ref, ob_ref, st_ref):
    @pl.when(pl.program_id(2) == 0)
    def _():
        st_ref[...] = jnp.zeros_like(st_ref)

    of_ref[...] = _hgrn_chunk(qf_ref[...], ff_ref[...], vf_ref[...], lb_ref[0, 0], st_ref.at[0], False)
    ob_ref[...] = _hgrn_chunk(qb_ref[...], fb_ref[...], vb_ref[...], lb_ref[1, 0], st_ref.at[1], True)


def _hgrn(proj3, lb4, chunk=HG_CHUNK):
    b, l, _ = proj3.shape
    n = l // chunk
    hpb = 512 // HG_DK

    def spec(col, back):
        if back:
            return pl.BlockSpec((None, chunk, HG_DK), lambda bi, h, i: (bi, n - 1 - i, col * hpb + h))
        return pl.BlockSpec((None, chunk, HG_DK), lambda bi, h, i: (bi, i, col * hpb + h))

    out_f = pl.BlockSpec((None, chunk, HG_DV), lambda bi, h, i: (bi, i, h))
    out_b = pl.BlockSpec((None, chunk, HG_DV), lambda bi, h, i: (bi, n - 1 - i, h))
    return pl.pallas_call(
        _hgrn_body,
        grid=(b, HG_HEADS, n),
        in_specs=[spec(COL_HG_Q, False), spec(COL_HG_FF, False), spec(COL_HG_V, False),
                  spec(COL_HG_Q, True), spec(COL_HG_FB, True), spec(COL_HG_V, True),
                  pl.BlockSpec((2, None, 1, HG_DK), lambda bi, h, i: (0, h, 0, 0))],
        out_specs=[out_f, out_b],
        out_shape=[jax.ShapeDtypeStruct((b, l, HG_HEADS * HG_DV), F32)] * 2,
        scratch_shapes=[pltpu.VMEM((2, HG_DV, HG_DK), F32)],
        compiler_params=_cparams(("parallel", "parallel", "arbitrary"), 32),
    )(*([proj3] * 6), lb4)


CONV_HALO = 16


def _conv_body(a_ref, b_ref, ap_ref, bp_ref, an_ref, bn_ref, w_ref, cb_ref, g_ref, be_ref, o_ref, u_ref):
    i = pl.program_id(1)
    n = pl.num_programs(1)
    t = a_ref.shape[0]

    def glu(a, b):
        return a[...].astype(F32) * _sigmoid(b[...].astype(F32))

    u_ref[pl.ds(0, CONV_HALO), :] = jnp.where(i > 0, glu(ap_ref, bp_ref), 0.0)
    u_ref[pl.ds(CONV_HALO, t), :] = glu(a_ref, b_ref)
    u_ref[pl.ds(CONV_HALO + t, CONV_HALO), :] = jnp.where(i < n - 1, glu(an_ref, bn_ref), 0.0)
    base = CONV_HALO - CONV_K // 2
    acc = jnp.zeros((t, D_CONV), F32)
    for kk in range(CONV_K):
        acc = acc + w_ref[kk:kk + 1, :] * u_ref[pl.ds(base + kk, t), :]
    y = acc + cb_ref[...]
    mu = jnp.mean(y, axis=-1, keepdims=True)
    yc = y - mu
    var = jnp.mean(yc * yc, axis=-1, keepdims=True)
    yn = yc * lax.rsqrt(var + EPS) * g_ref[...] + be_ref[...]
    o_ref[...] = (yn * _sigmoid(yn)).astype(BF16)


def _conv(proj3, conv_w, conv_b, ln_g, ln_b, t=512):
    b, l, _ = proj3.shape
    nt = l // t
    hb = t // CONV_HALO
    nh = l // CONV_HALO
    main = lambda col: pl.BlockSpec((None, t, D_CONV), lambda bi, i: (bi, i, col))
    prev = lambda col: pl.BlockSpec((None, CONV_HALO, D_CONV), lambda bi, i: (bi, jnp.maximum(i * hb - 1, 0), col))
    nxt = lambda col: pl.BlockSpec((None, CONV_HALO, D_CONV), lambda bi, i: (bi, jnp.minimum((i + 1) * hb, nh - 1), col))
    vec = pl.BlockSpec((1, D_CONV), lambda bi, i: (0, 0))
    return pl.pallas_call(
        _conv_body,
        grid=(b, nt),
        in_specs=[main(COL_CV_A), main(COL_CV_B), prev(COL_CV_A), prev(COL_CV_B), nxt(COL_CV_A), nxt(COL_CV_B),
                  pl.BlockSpec((CONV_K, D_CONV), lambda bi, i: (0, 0)), vec, vec, vec],
        out_specs=pl.BlockSpec((None, t, D_CONV), lambda bi, i: (bi, i, 0)),
        out_shape=jax.ShapeDtypeStruct((b, l, D_CONV), BF16),
        scratch_shapes=[pltpu.VMEM((t + 2 * CONV_HALO, D_CONV), F32)],
        compiler_params=_cparams(("parallel", "arbitrary"), 32),
    )(*([proj3] * 6), conv_w, conv_b, ln_g, ln_b)


def _merge_body(na_ref, of_ref, ob_ref, hg_ref, cv_ref, gna_ref, ghg_ref, gcv_ref, x_ref,
                og_ref, wna_ref, whg_ref, wcv_ref, wout_ref, o_ref):
    o = of_ref[...] + ob_ref[...]
    parts = []
    for h in range(HG_HEADS):
        oh = o[:, h * HG_DV:(h + 1) * HG_DV]
        parts.append(oh * lax.rsqrt(jnp.mean(oh * oh, axis=-1, keepdims=True) + EPS))
    gg = hg_ref[...].astype(F32)
    hg = jnp.concatenate(parts, axis=1) * og_ref[...] * (gg * _sigmoid(gg))
    merged = (_sigmoid(gna_ref[...].astype(F32)) * _dot(na_ref[...], wna_ref[...])
              + _sigmoid(ghg_ref[...].astype(F32)) * _dot(hg.astype(BF16), whg_ref[...])
              + _sigmoid(gcv_ref[...].astype(F32)) * _dot(cv_ref[...], wcv_ref[...]))
    o_ref[...] = x_ref[...] + _dot(merged.astype(BF16), wout_ref[...])


def _merge(na2, of2, ob2, proj2, cv2, x2, og, wna, whg, wcv, wout, t=512):
    n = x2.shape[0]
    half = lambda col=0: pl.BlockSpec((t, 512), lambda i: (i, col))
    full = lambda col=0: pl.BlockSpec((t, D_MODEL), lambda i: (i, col))
    wspec = lambda r: pl.BlockSpec((r, D_MODEL), lambda i: (0, 0))
    return pl.pallas_call(
        _merge_body,
        grid=(n // t,),
        in_specs=[half(), half(), half(), half(COL_HG_G), half(), full(COL_G_NA), full(COL_G_HG), full(COL_G_CV),
                  full(), pl.BlockSpec((1, 512), lambda i: (0, 0)), wspec(512), wspec(512), wspec(512),
                  wspec(D_MODEL)],
        out_specs=full(),
        out_shape=jax.ShapeDtypeStruct((n, D_MODEL), F32),
        compiler_params=_cparams(("parallel",), 48),
    )(na2, of2, ob2, proj2, cv2, proj2, proj2, proj2, x2, og, wna, whg, wcv, wout)


def _memkv_body(m_ref, g_ref, w_ref, kn_ref, o_ref):
    kv = _dot(_rms_rows(m_ref[...], g_ref[...]).astype(BF16), w_ref[...])
    parts = []
    for h in range(MEM_HEADS):
        kh = kv[:, h * MEM_HD:(h + 1) * MEM_HD]
        parts.append(_rms_rows(kh, kn_ref[...]))
    parts.append(kv[:, MEM_WIDTH:])
    o_ref[...] = jnp.concatenate(parts, axis=1).astype(BF16)


def _memkv(mem2, g, wkv, kn, t=256):
    n = mem2.shape[0]
    return pl.pallas_call(
        _memkv_body,
        grid=(n // t,),
        in_specs=[pl.BlockSpec((t, D_MODEL), lambda i: (i, 0)), pl.BlockSpec((1, D_MODEL), lambda i: (0, 0)),
                  pl.BlockSpec((D_MODEL, 2 * MEM_WIDTH), lambda i: (0, 0)),
                  pl.BlockSpec((1, MEM_HD), lambda i: (0, 0))],
        out_specs=pl.BlockSpec((t, 2 * MEM_WIDTH), lambda i: (i, 0)),
        out_shape=jax.ShapeDtypeStruct((n, 2 * MEM_WIDTH), BF16),
        compiler_params=_cparams(("parallel",), 32),
    )(mem2, g, wkv, kn)


def _memattn_body(x_ref, g_ref, wq_ref, qn_ref, kv_ref, wo_ref, o_ref):
    x = x_ref[...]
    q = _dot(_rms_rows(x, g_ref[...]).astype(BF16), wq_ref[...])
    kv = kv_ref[...]
    outs = []
    for h in range(MEM_HEADS):
        sl = slice(h * MEM_HD, (h + 1) * MEM_HD)
        qh = _rms_rows(q[:, sl], qn_ref[...]) * (MEM_HD ** -0.5)
        s = _dot_nt(qh.astype(BF16), kv[:, sl])
        m = jnp.max(s, axis=-1, keepdims=True)
        p = jnp.exp(s - m)
        l = jnp.sum(p, axis=-1, keepdims=True)
        outs.append(_dot(p.astype(BF16), kv[:, MEM_WIDTH + h * MEM_HD:MEM_WIDTH + (h + 1) * MEM_HD]) / l)
    o_ref[...] = x + _dot(jnp.concatenate(outs, axis=1).astype(BF16), wo_ref[...])


def _memattn(x3, g, wq, qn, kv3, wo, t=512):
    b, l, _ = x3.shape
    m = kv3.shape[1]
    return pl.pallas_call(
        _memattn_body,
        grid=(b, l // t),
        in_specs=[pl.BlockSpec((None, t, D_MODEL), lambda bi, i: (bi, i, 0)),
                  pl.BlockSpec((1, D_MODEL), lambda bi, i: (0, 0)),
                  pl.BlockSpec((D_MODEL, MEM_WIDTH), lambda bi, i: (0, 0)),
                  pl.BlockSpec((1, MEM_HD), lambda bi, i: (0, 0)),
                  pl.BlockSpec((None, m, 2 * MEM_WIDTH), lambda bi, i: (bi, 0, 0)),
                  pl.BlockSpec((MEM_WIDTH, D_MODEL), lambda bi, i: (0, 0))],
        out_specs=pl.BlockSpec((None, t, D_MODEL), lambda bi, i: (bi, i, 0)),
        out_shape=jax.ShapeDtypeStruct((b, l, D_MODEL), F32),
        compiler_params=_cparams(("parallel", "arbitrary"), 32),
    )(x3, g, wq, qn, kv3, wo)


def _router_body(x_ref, g_ref, w_ref, h_ref, a_ref):
    hn = _rms_rows(x_ref[...], g_ref[...])
    h_ref[...] = hn.astype(BF16)
    h_hi, h_lo = _split2(hn)
    w_hi, w_lo = _split2(w_ref[...])
    logits = _dot(h_hi, w_hi) + _dot(h_hi, w_lo) + _dot(h_lo, w_hi)
    m = jnp.max(logits, axis=-1, keepdims=True)
    e = jnp.exp(logits - m)
    a_ref[...] = e / jnp.sum(e, axis=-1, keepdims=True)


def _router(x2, g, w_router, t=512):
    n = x2.shape[0]
    return pl.pallas_call(
        _router_body,
        grid=(n // t,),
        in_specs=[pl.BlockSpec((t, D_MODEL), lambda i: (i, 0)), pl.BlockSpec((1, D_MODEL), lambda i: (0, 0)),
                  pl.BlockSpec((D_MODEL, N_EXPERTS), lambda i: (0, 0))],
        out_specs=[pl.BlockSpec((t, D_MODEL), lambda i: (i, 0)), pl.BlockSpec((t, N_EXPERTS), lambda i: (i, 0))],
        out_shape=[jax.ShapeDtypeStruct((n, D_MODEL), BF16), jax.ShapeDtypeStruct((n, N_EXPERTS), F32)],
        compiler_params=_cparams(("parallel",), 32),
    )(x2, g, w_router)


def _thresh_body(a_ref, thr_ref, need_ref, *, cap):
    bits = pltpu.bitcast(a_ref[...], I32)

    def step(b, thr):
        cand = thr | (1 << (30 - b))
        cnt = jnp.sum((bits >= cand).astype(I32), axis=1, keepdims=True)
        return jnp.where(cnt >= cap, cand, thr)

    thr = lax.fori_loop(0, 31, step, jnp.zeros((N_EXPERTS, 1), I32))
    thr_ref[...] = thr
    need_ref[...] = cap - jnp.sum((bits > thr).astype(I32), axis=1, keepdims=True)


def _thresh(aff_t, cap):
    n = aff_t.shape[1]
    return pl.pallas_call(
        functools.partial(_thresh_body, cap=cap),
        grid=(1,),
        in_specs=[pl.BlockSpec((N_EXPERTS, n), lambda i: (0, 0))],
        out_specs=[pl.BlockSpec((N_EXPERTS, 1), lambda i: (0, 0))] * 2,
        out_shape=[jax.ShapeDtypeStruct((N_EXPERTS, 1), I32)] * 2,
        compiler_params=_cparams(("arbitrary",), 32),
    )(aff_t)


def _slots_body(a_ref, thr_ref, need_ref, pos_ref, eq_off, sel_off):
    @pl.when(pl.program_id(0) == 0)
    def _():
        eq_off[...] = jnp.zeros_like(eq_off)
        sel_off[...] = jnp.zeros_like(sel_off)

    bits = pltpu.bitcast(a_ref[...], I32)
    w = bits.shape[1]
    thr = thr_ref[...]
    before = (lax.broadcasted_iota(I32, (w, w), 0) < lax.broadcasted_iota(I32, (w, w), 1)).astype(BF16)
    eq = bits == thr
    eq_f = eq.astype(F32)
    eq_rank = _dot(eq_f.astype(BF16), before) + eq_off[...]
    sel = (bits > thr) | (eq & (eq_rank < need_ref[...].astype(F32)))
    sel_f = sel.astype(F32)
    slot = _dot(sel_f.astype(BF16), before) + sel_off[...]
    pos_ref[...] = jnp.where(sel, slot, -1.0).astype(I32)
    eq_off[...] += jnp.sum(eq_f, axis=1, keepdims=True)
    sel_off[...] += jnp.sum(sel_f, axis=1, keepdims=True)


def _slots(aff_t, thr, need, w=TOK_BLK):
    n = aff_t.shape[1]
    col = pl.BlockSpec((N_EXPERTS, 1), lambda i: (0, 0))
    return pl.pallas_call(
        _slots_body,
        grid=(n // w,),
        in_specs=[pl.BlockSpec((N_EXPERTS, w), lambda i: (0, i)), col, col],
        out_specs=pl.BlockSpec((N_EXPERTS, w), lambda i: (0, i)),
        out_shape=jax.ShapeDtypeStruct((N_EXPERTS, n), I32),
        scratch_shapes=[pltpu.VMEM((N_EXPERTS, 1), F32)] * 2,
        compiler_params=_cparams(("arbitrary",), 32),
    )(aff_t, thr, need)


def _gather_body(cum_ref, h_ref, pos_ref, o_ref, *, cap):
    e = pl.program_id(0)
    sb = pl.program_id(1)
    nsub = SUPER // TOK_BLK

    @pl.when(sb == 0)
    def _():
        o_ref[...] = jnp.zeros_like(o_ref)

    slot_iota = lax.broadcasted_iota(I32, (SLOT_BLK, TOK_BLK), 0)
    for u in range(nsub):
        tb = sb * nsub + u
        lo = cum_ref[e, tb]
        hi = cum_ref[e, tb + 1]

        @pl.when(hi > lo)
        def _():
            hblk = h_ref[pl.ds(u * TOK_BLK, TOK_BLK), :]
            prow = pos_ref[:, pl.ds(u * TOK_BLK, TOK_BLK)]

            def one(j, carry):
                onehot = (slot_iota == (prow - j * SLOT_BLK)).astype(BF16)
                rows = pl.ds(pl.multiple_of(j * SLOT_BLK, SLOT_BLK), SLOT_BLK)
                o_ref[rows, :] += _dot(onehot, hblk).astype(BF16)
                return carry

            lax.fori_loop(lo // SLOT_BLK, (hi - 1) // SLOT_BLK + 1, one, 0)


def _gather(cum, hn, pos3, cap):
    n = hn.shape[0]
    return pl.pallas_call(
        functools.partial(_gather_body, cap=cap),
        grid_spec=pltpu.PrefetchScalarGridSpec(
            num_scalar_prefetch=1,
            grid=(N_EXPERTS, n // SUPER),
            in_specs=[pl.BlockSpec((SUPER, D_MODEL), lambda e, sb, cum: (sb, 0)),
                      pl.BlockSpec((None, 1, SUPER), lambda e, sb, cum: (e, 0, sb))],
            out_specs=pl.BlockSpec((None, cap, D_MODEL), lambda e, sb, cum: (e, 0, 0)),
        ),
        out_shape=jax.ShapeDtypeStruct((N_EXPERTS, cap, D_MODEL), BF16),
        compiler_params=_cparams(("parallel", "arbitrary"), 48),
    )(cum, hn, pos3)


def _ffn_body(x_ref, wg_ref, wu_ref, wd_ref, o_ref, acc_ref):
    f = pl.program_id(2)
    x = x_ref[...]
    gate = _dot(x, wg_ref[...])
    up = _dot(x, wu_ref[...])
    part = _dot((gate * _sigmoid(gate) * up).astype(BF16), wd_ref[...])

    @pl.when(f == 0)
    def _():
        acc_ref[...] = part

    @pl.when(f != 0)
    def _():
        acc_ref[...] += part

    @pl.when(f == pl.num_programs(2) - 1)
    def _():
        o_ref[...] = acc_ref[...].astype(BF16)


def _ffn(xe, wg, wu, wd, tm=1024, tf=256):
    e, cap, _ = xe.shape
    tm = min(tm, cap)
    return pl.pallas_call(
        _ffn_body,
        grid=(e, cap // tm, D_FF_PAD // tf),
        in_specs=[pl.BlockSpec((None, tm, D_MODEL), lambda ei, i, f: (ei, i, 0)),
                  pl.BlockSpec((None, D_MODEL, tf), lambda ei, i, f: (ei, 0, f)),
                  pl.BlockSpec((None, D_MODEL, tf), lambda ei, i, f: (ei, 0, f)),
                  pl.BlockSpec((None, tf, D_MODEL), lambda ei, i, f: (ei, f, 0))],
        out_specs=pl.BlockSpec((None, tm, D_MODEL), lambda ei, i, f: (ei, i, 0)),
        out_shape=jax.ShapeDtypeStruct((e, cap, D_MODEL), BF16),
        scratch_shapes=[pltpu.VMEM((tm, D_MODEL), F32)],
        compiler_params=_cparams(("parallel", "parallel", "arbitrary"), 48),
    )(xe, wg, wu, wd)


def _combine_body(cum_ref, x_ref, aff_ref, pos_ref, y_ref, o_ref):
    sb = pl.program_id(0)
    e = pl.program_id(1)
    nsub = SUPER // TOK_BLK

    @pl.when(e == 0)
    def _():
        o_ref[...] = x_ref[...]

    lane = lax.broadcasted_iota(I32, (TOK_BLK, N_EXPERTS), 1)
    slot_iota = lax.broadcasted_iota(I32, (TOK_BLK, SLOT_BLK), 1)
    for u in range(nsub):
        tb = sb * nsub + u
        lo = cum_ref[e, tb]
        hi = cum_ref[e, tb + 1]

        @pl.when(hi > lo)
        def _():
            rows = pl.ds(u * TOK_BLK, TOK_BLK)
            pcol = jnp.sum(jnp.where(lane == e, pos_ref[rows, :], 0), axis=1, keepdims=True)
            gcol = jnp.sum(jnp.where(lane == e, aff_ref[rows, :], 0.0), axis=1, keepdims=True)

            def one(j, acc):
                onehot = (slot_iota == (pcol - j * SLOT_BLK)).astype(BF16)
                yblk = y_ref[pl.ds(pl.multiple_of(j * SLOT_BLK, SLOT_BLK), SLOT_BLK), :]
                return acc + _dot(onehot, yblk)

            acc = lax.fori_loop(lo // SLOT_BLK, (hi - 1) // SLOT_BLK + 1, one,
                                jnp.zeros((TOK_BLK, D_MODEL), F32))
            o_ref[rows, :] += gcol * acc


def _combine(cum, x2, aff, pos_t, ye):
    n = x2.shape[0]
    cap = ye.shape[1]
    return pl.pallas_call(
        _combine_body,
        grid_spec=pltpu.PrefetchScalarGridSpec(
            num_scalar_prefetch=1,
            grid=(n // SUPER, N_EXPERTS),
            in_specs=[pl.BlockSpec((SUPER, D_MODEL), lambda sb, e, cum: (sb, 0)),
                      pl.BlockSpec((SUPER, N_EXPERTS), lambda sb, e, cum: (sb, 0)),
                      pl.BlockSpec((SUPER, N_EXPERTS), lambda sb, e, cum: (sb, 0)),
                      pl.BlockSpec((None, cap, D_MODEL), lambda sb, e, cum: (e, 0, 0))],
            out_specs=pl.BlockSpec((SUPER, D_MODEL), lambda sb, e, cum: (sb, 0)),
        ),
        out_shape=jax.ShapeDtypeStruct((n, D_MODEL), F32),
        compiler_params=_cparams(("parallel", "arbitrary"), 56),
    )(cum, x2, aff, pos_t, ye)


def _expert_choice(x2, g, w_router, wg, wu, wd):
    n = x2.shape[0]
    cap = EC_CAPACITY * n // N_EXPERTS
    hn, aff = _router(x2, g, w_router)
    aff_t = aff.T
    thr, need = _thresh(aff_t, cap)
    pos = _slots(aff_t, thr, need)
    cnt = jnp.sum((pos >= 0).reshape(N_EXPERTS, n // TOK_BLK, TOK_BLK), axis=-1, dtype=I32)
    cum = jnp.concatenate([jnp.zeros((N_EXPERTS, 1), I32), jnp.cumsum(cnt, axis=1, dtype=I32)], axis=1)
    xe = _gather(cum, hn, pos.reshape(N_EXPERTS, 1, n), cap)
    ye = _ffn(xe, wg, wu, wd)
    return _combine(cum, x2, aff, pos.T, ye)


def _prep_layer(l, p, lb):
    row = lambda a: a.astype(F32).reshape(1, -1)
    qkg = jnp.concatenate([jnp.tile(p["na_q_norm"][l] * (NA_HD ** -0.5), NA_HEADS),
                           jnp.tile(p["na_k_norm"][l], NA_HEADS)]).reshape(1, 2 * NA_WIDTH)
    idx = jnp.arange(2 * NA_WIDTH) // NA_HD
    pad = D_FF_PAD - D_FF
    wgu = p["w_gate_up"][l]
    return dict(
        norm_mix=row(p["norm_mix"][l]), w_in=p["w_in"][l].astype(BF16),
        grp=(idx[:, None] == idx[None, :]).astype(BF16), qkg=qkg.astype(F32),
        bias_tbl=_na_bias_table(p["na_rel_bias"][l]),
        lb4=lb[:, l].reshape(2, HG_HEADS, 1, HG_DK),
        conv_w=p["conv_w"][l].astype(F32), conv_b=row(p["conv_b"][l]),
        ln_g=row(p["conv_ln_g"][l]), ln_b=row(p["conv_ln_b"][l]),
        og=jnp.tile(p["hg_out_norm"][l], HG_HEADS).reshape(1, -1).astype(F32),
        wna=p["w_na_br"][l].astype(BF16), whg=p["w_hg_br"][l].astype(BF16), wcv=p["w_cv_br"][l].astype(BF16),
        wout=p["w_out"][l].astype(BF16),
        norm_mem=row(p["norm_mem"][l]), mem_kv_norm=row(p["mem_kv_norm"][l]),
        wq=p["wq_mem"][l].astype(BF16), wkv=p["wkv_mem"][l].astype(BF16),
        qn=row(p["mem_q_norm"][l]), kn=row(p["mem_k_norm"][l]), wo=p["wo_mem"][l].astype(BF16),
        norm_ffn=row(p["norm_ffn"][l]), w_router=p["w_router"][l].astype(F32),
        wg=jnp.pad(wgu[:, :, :D_FF], ((0, 0), (0, 0), (0, pad))).astype(BF16),
        wu=jnp.pad(wgu[:, :, D_FF:], ((0, 0), (0, 0), (0, pad))).astype(BF16),
        wd=jnp.pad(p["w_down"][l], ((0, 0), (0, pad), (0, 0))).astype(BF16),
    )


def _layer(x3, mem3, w):
    b, l, _ = x3.shape
    n = b * l
    x2 = x3.reshape(n, D_MODEL)
    proj2 = _inproj(x2, w["norm_mix"], w["w_in"], w["grp"], w["qkg"])
    proj3 = proj2.reshape(b, l, D_IN)
    na = _na(proj3, w["bias_tbl"])
    o_f, o_b = _hgrn(proj3, w["lb4"])
    cv = _conv(proj3, w["conv_w"], w["conv_b"], w["ln_g"], w["ln_b"])
    x2 = _merge(na.reshape(n, -1), o_f.reshape(n, -1), o_b.reshape(n, -1), proj2, cv.reshape(n, -1), x2,
                w["og"], w["wna"], w["whg"], w["wcv"], w["wout"])
    kv = _memkv(mem3.reshape(-1, D_MODEL), w["mem_kv_norm"], w["wkv"], w["kn"])
    x3 = _memattn(x2.reshape(b, l, D_MODEL), w["norm_mem"], w["wq"], w["qn"],
                  kv.reshape(b, -1, 2 * MEM_WIDTH), w["wo"])
    x2 = _expert_choice(x3.reshape(n, D_MODEL), w["norm_ffn"], w["w_router"], w["wg"], w["wu"], w["wd"])
    return x2.reshape(b, l, D_MODEL)


def kernel(x_prompt, x_sample, mem_prompt, mem_sample, norm_mix, w_in, na_q_norm, na_k_norm, na_rel_bias, w_na_br, hg_lb, hg_out_norm, w_hg_br, conv_w, conv_b, conv_ln_g, conv_ln_b, w_cv_br, w_out, norm_mem, mem_kv_norm, wq_mem, wkv_mem, mem_q_norm, mem_k_norm, wo_mem, norm_ffn, w_router, w_gate_up, w_down):
    p = dict(norm_mix=norm_mix, w_in=w_in, na_q_norm=na_q_norm, na_k_norm=na_k_norm, na_rel_bias=na_rel_bias,
             w_na_br=w_na_br, hg_out_norm=hg_out_norm, w_hg_br=w_hg_br, conv_w=conv_w, conv_b=conv_b,
             conv_ln_g=conv_ln_g, conv_ln_b=conv_ln_b, w_cv_br=w_cv_br, w_out=w_out, norm_mem=norm_mem,
             mem_kv_norm=mem_kv_norm, wq_mem=wq_mem, wkv_mem=wkv_mem, mem_q_norm=mem_q_norm,
             mem_k_norm=mem_k_norm, wo_mem=wo_mem, norm_ffn=norm_ffn, w_router=w_router, w_gate_up=w_gate_up,
             w_down=w_down)
    depth = w_in.shape[0]
    sm = jax.nn.softmax(hg_lb.astype(F32), axis=1)
    lb = jnp.cumsum(sm, axis=1) - sm[:, :1]
    xs = [x_prompt, x_sample]
    mems = [mem_prompt, mem_sample]
    for l in range(depth):
        w = _prep_layer(l, p, lb)
        xs = [_layer(x, m, w) for x, m in zip(xs, mems)]
    return tuple(xs)
```

```python
import functools

import jax
import jax.numpy as jnp
import numpy as np
from jax import lax
from jax.experimental import pallas as pl
from jax.experimental.pallas import tpu as pltpu

F32 = jnp.float32
BF16 = jnp.bfloat16
I32 = jnp.int32

D_MODEL = 1024
GRID_W = 64
NA_HEADS = 8
NA_HD = 64
NA_WIDTH = NA_HEADS * NA_HD
WIN_R = 8
WIN_C = 16
HG_HEADS = 4
HG_DK = 128
HG_DV = 128
HG_KW = HG_HEADS * HG_DK
F_MIN = 1e-20
D_CONV = 512
CONV_K = 31
MEM_HEADS = 4
MEM_HD = 128
MEM_WIDTH = MEM_HEADS * MEM_HD
N_EXPERTS = 16
EC_CAPACITY = 2
D_FF = 2752
EPS = 1e-6
D_IN = 8192

COL_NA_Q, COL_NA_K, COL_NA_V = 0, 1, 2
COL_HG_Q, COL_HG_FF, COL_HG_FB, COL_HG_V, COL_HG_G = 3, 4, 5, 6, 7
COL_CV_A, COL_CV_B = 8, 9
COL_G_NA, COL_G_HG, COL_G_CV = 5, 6, 7

LANES = 128
SUBLANES = 8
MXU_DIM = 256
D_FF_PAD = 2816
NEG_BIG = -1e30
LOG2E = 1.4426950408889634

HG_CHUNK = 256
SLOT_BLK = 256
TOK_BLK = 256
SUPER = 2048


def _cparams(sem, vmem_mb):
    return pltpu.CompilerParams(dimension_semantics=sem, vmem_limit_bytes=vmem_mb << 20)


def _sigmoid_pair(x):
    t = 0.5 * jnp.tanh(0.5 * x)
    return 0.5 + t, 0.5 - t


def _sigmoid(x):
    return 0.5 + 0.5 * jnp.tanh(0.5 * x)


def _neg_abs(x):
    return lax.bitcast_convert_type(lax.bitcast_convert_type(x, I32) | jnp.int32(-2 ** 31), F32)


def _split2(a):
    hi = a.astype(BF16)
    lo = (a - hi.astype(F32)).astype(BF16)
    return hi, lo


def _dot(a, b):
    return jnp.dot(a, b, preferred_element_type=F32)


def _dot_nt(a, b):
    return lax.dot_general(a, b, (((1,), (1,)), ((), ())), preferred_element_type=F32)


def _dot_tn(a, b):
    return lax.dot_general(a, b, (((0,), (0,)), ((), ())), preferred_element_type=F32)


def _rms_rows(x, g):
    return x * lax.rsqrt(jnp.mean(x * x, axis=-1, keepdims=True) + EPS) * g


def _inproj_body(x_ref, g_ref, w_ref, grp_ref, qkg_ref, o_ref, h_ref):
    j = pl.program_id(1)

    @pl.when(j == 0)
    def _():
        h_ref[...] = _rms_rows(x_ref[...], g_ref[...]).astype(BF16)

    acc = _dot(h_ref[...], w_ref[...])

    @pl.when(j == 0)
    def _():
        sq = (acc * acc).astype(BF16)
        w = grp_ref.shape[0]
        ss = jnp.concatenate([_dot(sq[:, c * w:(c + 1) * w], grp_ref[...]) for c in range(sq.shape[1] // w)], axis=1)
        o_ref[...] = (acc * lax.rsqrt(ss * (1.0 / NA_HD) + EPS) * qkg_ref[...]).astype(BF16)

    @pl.when(j != 0)
    def _():
        o_ref[...] = acc.astype(BF16)


def _inproj(x2, g, w_bf, grp, qkg, tm=1024, tn=1024):
    n = x2.shape[0]
    return pl.pallas_call(
        _inproj_body,
        grid=(n // tm, D_IN // tn),
        in_specs=[
            pl.BlockSpec((tm, D_MODEL), lambda i, j: (i, 0)),
            pl.BlockSpec((1, D_MODEL), lambda i, j: (0, 0)),
            pl.BlockSpec((D_MODEL, tn), lambda i, j: (0, j)),
            pl.BlockSpec((MXU_DIM, MXU_DIM), lambda i, j: (0, 0)),
            pl.BlockSpec((1, tn), lambda i, j: (0, 0)),
        ],
        out_specs=pl.BlockSpec((tm, tn), lambda i, j: (i, j)),
        out_shape=jax.ShapeDtypeStruct((n, D_IN), BF16),
        scratch_shapes=[pltpu.VMEM((tm, D_MODEL), BF16)],
        compiler_params=_cparams(("parallel", "arbitrary"), 48),
        name="in_proj",
    )(x2, g, w_bf, grp, qkg)


NA_QROWS = 4
NA_KROWS = NA_QROWS + WIN_R


def _na_body(q_ref, k_ref, v_ref, bias_ref, o_ref):
    q = q_ref[...]
    k = k_ref[0]
    v = v_ref[0]
    outs = []
    for h in range(NA_HEADS):
        sl = slice(h * NA_HD, (h + 1) * NA_HD)
        s = _dot_nt(q[:, sl], k[:, sl]) + bias_ref[h]
        m = jnp.max(s, axis=-1, keepdims=True)
        p = jnp.exp(s - m)
        l = jnp.sum(p, axis=-1, keepdims=True)
        outs.append(_dot(p.astype(BF16), v[:, sl]) / l)
    o_ref[...] = jnp.concatenate(outs, axis=1).astype(BF16)


def _na(proj3, bias_tbl):
    b, l, _ = proj3.shape
    rows = l // GRID_W
    assert rows % NA_QROWS == 0 and rows >= 2 * WIN_R
    nblk = rows // NA_QROWS

    def key_row0(g):
        return jnp.clip(NA_QROWS * g - WIN_R // 2, 0, rows - NA_KROWS)

    def variant(g):
        return jnp.where(g == 0, 0, jnp.where(g == nblk - 1, 2, 1))

    qblk = (None, NA_QROWS * GRID_W, NA_WIDTH)
    kblk = (pl.Element(1), pl.Element(NA_KROWS * GRID_W), pl.Element(NA_WIDTH))
    return pl.pallas_call(
        _na_body,
        grid=(b, nblk),
        in_specs=[pl.BlockSpec(qblk, lambda bi, g: (bi, g, COL_NA_Q)),
                  pl.BlockSpec(kblk, lambda bi, g: (bi, key_row0(g) * GRID_W, COL_NA_K * NA_WIDTH)),
                  pl.BlockSpec(kblk, lambda bi, g: (bi, key_row0(g) * GRID_W, COL_NA_V * NA_WIDTH)),
                  pl.BlockSpec((None, NA_HEADS, NA_QROWS * GRID_W, NA_KROWS * GRID_W),
                               lambda bi, g: (variant(g), 0, 0, 0))],
        out_specs=pl.BlockSpec(qblk, lambda bi, g: (bi, g, 0)),
        out_shape=jax.ShapeDtypeStruct((b, l, NA_WIDTH), BF16),
        compiler_params=_cparams(("parallel", "arbitrary"), 48),
        name="na_attention",
    )(proj3, proj3, proj3, bias_tbl)


def _na_bias_table(rel_bias):
    c = np.arange(GRID_W)
    c0 = np.clip(c - WIN_C // 2, 0, GRID_W - WIN_C)
    kc = np.arange(GRID_W)
    col_ok = (kc[None, :] >= c0[:, None]) & (kc[None, :] < c0[:, None] + WIN_C)
    dcol = kc[None, :] - c[:, None] + WIN_C - 1
    pick = (dcol[:, :, None] == np.arange(2 * WIN_C - 1)[None, None, :]) & col_ok[:, :, None]
    toe = jnp.einsum('hrd,ckd->hrck', rel_bias.astype(F32), jnp.asarray(pick, F32),
                     precision=lax.Precision.HIGHEST)
    toe = jnp.where(jnp.asarray(col_ok)[None, None], toe, NEG_BIG)
    pad = NA_KROWS
    toe = jnp.pad(toe, ((0, 0), (pad, pad), (0, 0), (0, 0)), constant_values=NEG_BIG)
    i = np.arange(NA_KROWS)
    variants = []
    for var in range(3):
        per_row = []
        for rho in range(NA_QROWS):
            off = {0: rho, 1: rho + WIN_R // 2, 2: rho + WIN_R}[var]
            lo = {0: 0, 1: rho, 2: NA_QROWS}[var]
            row_ok = (i >= lo) & (i < lo + WIN_R)
            dr0 = -off + WIN_R - 1 + pad
            t = toe[:, dr0:dr0 + NA_KROWS]
            t = jnp.where(jnp.asarray(row_ok)[None, :, None, None], t, NEG_BIG)
            per_row.append(t.transpose(0, 2, 1, 3).reshape(NA_HEADS, GRID_W, NA_KROWS * GRID_W))
        variants.append(jnp.concatenate(per_row, axis=1))
    return jnp.stack(variants)


def _hgrn_chunk(hq, hf, v, lb, st_ref, rev):
    c = hq.shape[0]
    hq = hq.astype(F32)
    q = hq * _sigmoid(hq) * (HG_DK ** -0.5)
    s_pos, s_neg = _sigmoid_pair(hf.astype(F32))
    f = lb + (1.0 - lb) * s_pos
    logf = jnp.log(jnp.maximum(f, F_MIN))
    k = (1.0 - lb) * s_neg

    hb = c // 2
    t_col = lax.broadcasted_iota(I32, (c, 1), 0)
    t_mat = lax.broadcasted_iota(I32, (c, c), 0)
    s_mat = lax.broadcasted_iota(I32, (c, c), 1)
    tri = (s_mat >= t_mat) if rev else (s_mat <= t_mat)
    hi, lo = _split2(logf)
    cs = _dot(tri.astype(BF16), jnp.concatenate([hi, lo], axis=1))
    a = (cs[:, :HG_DK] + cs[:, HG_DK:]) * LOG2E

    st = st_ref[...]
    o_state = _dot_nt((q * jnp.exp2(a)).astype(BF16), st.astype(BF16))

    tl = lax.broadcasted_iota(I32, (hb, hb), 0)
    sl = lax.broadcasted_iota(I32, (hb, hb), 1)
    x = tl ^ sl
    keep = (tl <= sl) if rev else (tl >= sl)
    dg = jnp.sum(q * k, axis=1, keepdims=True)
    att = [jnp.where(x == 0, dg[b * hb:(b + 1) * hb], 0.0) for b in range(2)]
    a8 = a.reshape(c // 8, 8, HG_DK)
    sub = lax.broadcasted_iota(I32, (1, 8, 1), 1)
    m = 1
    while m < hb:
        if m >= 4:
            ar = a.reshape(c // (2 * m), 2 * m, HG_DK)
            row = m if rev else m - 1
            ab = jnp.broadcast_to(ar[:, row:row + 1, :], ar.shape).reshape(c, HG_DK)
        else:
            off = m if rev else m - 1
            rows = [a8[:, p + off:p + off + 1, :] for p in range(0, 8, 2 * m)]
            ab = rows[-1]
            for idx in range(len(rows) - 2, -1, -1):
                ab = jnp.where(sub < (idx + 1) * 2 * m, rows[idx], ab)
            ab = jnp.broadcast_to(ab, a8.shape).reshape(c, HG_DK)
        e = jnp.exp2(_neg_abs(a - ab))
        upper = (t_col & m) != 0
        q_half = jnp.logical_not(upper) if rev else upper
        z = (jnp.where(q_half, q, k) * e).astype(BF16)
        for b in range(2):
            zb = z[b * hb:(b + 1) * hb]
            att[b] = jnp.where(x >= m, _dot_nt(zb, zb), att[b])
        m *= 2
    att = [jnp.where(keep, t, 0.0).astype(BF16) for t in att]

    qs, ks = (slice(0, hb), slice(hb, c)) if rev else (slice(hb, c), slice(0, hb))
    ab = a[hb:hb + 1] if rev else a[hb - 1:hb]
    qe = (q[qs] * jnp.exp2(_neg_abs(a[qs] - ab))).astype(BF16)
    ke = (k[ks] * jnp.exp2(_neg_abs(a[ks] - ab))).astype(BF16)
    cross = _dot(_dot_nt(qe, ke).astype(BF16), v[ks])
    o0 = _dot(att[0], v[:hb])
    o1 = _dot(att[1], v[hb:])
    o = o_state + (jnp.concatenate([o0 + cross, o1], axis=0) if rev else jnp.concatenate([o0, o1 + cross], axis=0))

    a_end = a[0:1, :] if rev else a[c - 1:c, :]
    k_end = (k * jnp.exp2(a_end - a)).astype(BF16)
    st_ref[...] = st * jnp.exp2(a_end) + _dot_tn(v, k_end)
    return o


def _hgrn_body(qf_ref, ff_ref, vf_ref, qb_ref, fb_ref, vb_ref, lb_ref, of_ref, ob_ref, st_ref):
    @pl.when(pl.program_id(2) == 0)
    def _():
        st_ref[...] = jnp.zeros_like(st_ref)

    of_ref[...] = _hgrn_chunk(qf_ref[...], ff_ref[...], vf_ref[...], lb_ref[0, 0], st_ref.at[0], False)
    ob_ref[...] = _hgrn_chunk(qb_ref[...], fb_ref[...], vb_ref[...], lb_ref[1, 0], st_ref.at[1], True)


def _hgrn(proj3, lb4, chunk=HG_CHUNK):
    b, l, _ = proj3.shape
    n = l // chunk
    hpb = 512 // HG_DK

    def spec(col, back):
        if back:
            return pl.BlockSpec((None, chunk, HG_DK), lambda bi, h, i: (bi, n - 1 - i, col * hpb + h))
        return pl.BlockSpec((None, chunk, HG_DK), lambda bi, h, i: (bi, i, col * hpb + h))

    out_f = pl.BlockSpec((None, chunk, HG_DV), lambda bi, h, i: (bi, i, h))
    out_b = pl.BlockSpec((None, chunk, HG_DV), lambda bi, h, i: (bi, n - 1 - i, h))
    return pl.pallas_call(
        _hgrn_body,
        grid=(b, HG_HEADS, n),
        in_specs=[spec(COL_HG_Q, False), spec(COL_HG_FF, False), spec(COL_HG_V, False),
                  spec(COL_HG_Q, True), spec(COL_HG_FB, True), spec(COL_HG_V, True),
                  pl.BlockSpec((2, None, 1, HG_DK), lambda bi, h, i: (0, h, 0, 0))],
        out_specs=[out_f, out_b],
        out_shape=[jax.ShapeDtypeStruct((b, l, HG_HEADS * HG_DV), F32)] * 2,
        scratch_shapes=[pltpu.VMEM((2, HG_DV, HG_DK), F32)],
        compiler_params=_cparams(("parallel", "parallel", "arbitrary"), 32),
        name="hgrn2",
    )(*([proj3] * 6), lb4)


CONV_HALO = 16


def _conv_body(a_ref, b_ref, ap_ref, bp_ref, an_ref, bn_ref, w_ref, cb_ref, g_ref, be_ref, o_ref, u_ref, sh_ref):
    i = pl.program_id(1)
    n = pl.num_programs(1)
    t = a_ref.shape[0]

    def glu(a, b):
        return a[...].astype(F32) * _sigmoid(b[...].astype(F32))

    u_ref[pl.ds(0, CONV_HALO), :] = jnp.where(i > 0, glu(ap_ref, bp_ref), 0.0)
    u_ref[pl.ds(CONV_HALO, t), :] = glu(a_ref, b_ref)
    u_ref[pl.ds(CONV_HALO + t, CONV_HALO), :] = jnp.where(i < n - 1, glu(an_ref, bn_ref), 0.0)
    span = t + 2 * CONV_HALO - SUBLANES
    for r in range(1, SUBLANES):
        sh_ref[r - 1] = u_ref[pl.ds(r, span), :]
    base = CONV_HALO - CONV_K // 2
    acc = jnp.zeros((t, D_CONV), F32)
    for kk in range(CONV_K):
        off = base + kk
        r = off % SUBLANES
        win = u_ref[pl.ds(off, t), :] if r == 0 else sh_ref[r - 1, pl.ds(off - r, t), :]
        acc = acc + w_ref[kk:kk + 1, :] * win
    y = acc + cb_ref[...]
    mu = jnp.mean(y, axis=-1, keepdims=True)
    yc = y - mu
    var = jnp.mean(yc * yc, axis=-1, keepdims=True)
    yn = yc * lax.rsqrt(var + EPS) * g_ref[...] + be_ref[...]
    o_ref[...] = (yn * _sigmoid(yn)).astype(BF16)


def _conv(proj3, conv_w, conv_b, ln_g, ln_b, t=512):
    b, l, _ = proj3.shape
    nt = l // t
    hb = t // CONV_HALO
    nh = l // CONV_HALO
    main = lambda col: pl.BlockSpec((None, t, D_CONV), lambda bi, i: (bi, i, col))
    prev = lambda col: pl.BlockSpec((None, CONV_HALO, D_CONV), lambda bi, i: (bi, jnp.maximum(i * hb - 1, 0), col))
    nxt = lambda col: pl.BlockSpec((None, CONV_HALO, D_CONV), lambda bi, i: (bi, jnp.minimum((i + 1) * hb, nh - 1), col))
    vec = pl.BlockSpec((1, D_CONV), lambda bi, i: (0, 0))
    return pl.pallas_call(
        _conv_body,
        grid=(b, nt),
        in_specs=[main(COL_CV_A), main(COL_CV_B), prev(COL_CV_A), prev(COL_CV_B), nxt(COL_CV_A), nxt(COL_CV_B),
                  pl.BlockSpec((CONV_K, D_CONV), lambda bi, i: (0, 0)), vec, vec, vec],
        out_specs=pl.BlockSpec((None, t, D_CONV), lambda bi, i: (bi, i, 0)),
        out_shape=jax.ShapeDtypeStruct((b, l, D_CONV), BF16),
        scratch_shapes=[pltpu.VMEM((t + 2 * CONV_HALO, D_CONV), F32),
                        pltpu.VMEM((SUBLANES - 1, t + 2 * CONV_HALO - SUBLANES, D_CONV), F32)],
        compiler_params=_cparams(("parallel", "arbitrary"), 32),
        name="conv_module",
    )(*([proj3] * 6), conv_w, conv_b, ln_g, ln_b)


def _merge_body(na_ref, of_ref, ob_ref, hg_ref, cv_ref, gna_ref, ghg_ref, gcv_ref, x_ref,
                og_ref, wna_ref, whg_ref, wcv_ref, wout_ref, o_ref):
    o = of_ref[...] + ob_ref[...]
    parts = []
    for h in range(HG_HEADS):
        oh = o[:, h * HG_DV:(h + 1) * HG_DV]
        parts.append(oh * lax.rsqrt(jnp.mean(oh * oh, axis=-1, keepdims=True) + EPS))
    gg = hg_ref[...].astype(F32)
    hg = jnp.concatenate(parts, axis=1) * og_ref[...] * (gg * _sigmoid(gg))
    merged = (_sigmoid(gna_ref[...].astype(F32)) * _dot(na_ref[...], wna_ref[...])
              + _sigmoid(ghg_ref[...].astype(F32)) * _dot(hg.astype(BF16), whg_ref[...])
              + _sigmoid(gcv_ref[...].astype(F32)) * _dot(cv_ref[...], wcv_ref[...]))
    o_ref[...] = x_ref[...] + _dot(merged.astype(BF16), wout_ref[...])


def _merge(na2, of2, ob2, proj2, cv2, x2, og, wna, whg, wcv, wout, t=512):
    n = x2.shape[0]
    half = lambda col=0: pl.BlockSpec((t, 512), lambda i: (i, col))
    full = lambda col=0: pl.BlockSpec((t, D_MODEL), lambda i: (i, col))
    wspec = lambda r: pl.BlockSpec((r, D_MODEL), lambda i: (0, 0))
    return pl.pallas_call(
        _merge_body,
        grid=(n // t,),
        in_specs=[half(), half(), half(), half(COL_HG_G), half(), full(COL_G_NA), full(COL_G_HG), full(COL_G_CV),
                  full(), pl.BlockSpec((1, 512), lambda i: (0, 0)), wspec(512), wspec(512), wspec(512),
                  wspec(D_MODEL)],
        out_specs=full(),
        out_shape=jax.ShapeDtypeStruct((n, D_MODEL), F32),
        compiler_params=_cparams(("parallel",), 48),
        name="merge_out_proj",
    )(na2, of2, ob2, proj2, cv2, proj2, proj2, proj2, x2, og, wna, whg, wcv, wout)


def _memkv_body(m_ref, g_ref, w_ref, kn_ref, o_ref):
    kv = _dot(_rms_rows(m_ref[...], g_ref[...]).astype(BF16), w_ref[...])
    parts = []
    for h in range(MEM_HEADS):
        kh = kv[:, h * MEM_HD:(h + 1) * MEM_HD]
        parts.append(_rms_rows(kh, kn_ref[...]))
    parts.append(kv[:, MEM_WIDTH:])
    o_ref[...] = jnp.concatenate(parts, axis=1).astype(BF16)


def _memkv(mem2, g, wkv, kn, t=256):
    n = mem2.shape[0]
    return pl.pallas_call(
        _memkv_body,
        grid=(n // t,),
        in_specs=[pl.BlockSpec((t, D_MODEL), lambda i: (i, 0)), pl.BlockSpec((1, D_MODEL), lambda i: (0, 0)),
                  pl.BlockSpec((D_MODEL, 2 * MEM_WIDTH), lambda i: (0, 0)),
                  pl.BlockSpec((1, MEM_HD), lambda i: (0, 0))],
        out_specs=pl.BlockSpec((t, 2 * MEM_WIDTH), lambda i: (i, 0)),
        out_shape=jax.ShapeDtypeStruct((n, 2 * MEM_WIDTH), BF16),
        compiler_params=_cparams(("parallel",), 32),
        name="mem_kv",
    )(mem2, g, wkv, kn)


def _memattn_body(x_ref, g_ref, wq_ref, qn_ref, kv_ref, wo_ref, o_ref):
    x = x_ref[...]
    q = _dot(_rms_rows(x, g_ref[...]).astype(BF16), wq_ref[...])
    kv = kv_ref[...]
    outs = []
    for h in range(MEM_HEADS):
        sl = slice(h * MEM_HD, (h + 1) * MEM_HD)
        qh = _rms_rows(q[:, sl], qn_ref[...]) * (MEM_HD ** -0.5)
        s = _dot_nt(qh.astype(BF16), kv[:, sl])
        m = jnp.max(s, axis=-1, keepdims=True)
        p = jnp.exp(s - m)
        l = jnp.sum(p, axis=-1, keepdims=True)
        outs.append(_dot(p.astype(BF16), kv[:, MEM_WIDTH + h * MEM_HD:MEM_WIDTH + (h + 1) * MEM_HD]) / l)
    o_ref[...] = x + _dot(jnp.concatenate(outs, axis=1).astype(BF16), wo_ref[...])


def _memattn(x3, g, wq, qn, kv3, wo, t=512):
    b, l, _ = x3.shape
    m = kv3.shape[1]
    return pl.pallas_call(
        _memattn_body,
        grid=(b, l // t),
        in_specs=[pl.BlockSpec((None, t, D_MODEL), lambda bi, i: (bi, i, 0)),
                  pl.BlockSpec((1, D_MODEL), lambda bi, i: (0, 0)),
                  pl.BlockSpec((D_MODEL, MEM_WIDTH), lambda bi, i: (0, 0)),
                  pl.BlockSpec((1, MEM_HD), lambda bi, i: (0, 0)),
                  pl.BlockSpec((None, m, 2 * MEM_WIDTH), lambda bi, i: (bi, 0, 0)),
                  pl.BlockSpec((MEM_WIDTH, D_MODEL), lambda bi, i: (0, 0))],
        out_specs=pl.BlockSpec((None, t, D_MODEL), lambda bi, i: (bi, i, 0)),
        out_shape=jax.ShapeDtypeStruct((b, l, D_MODEL), F32),
        compiler_params=_cparams(("parallel", "arbitrary"), 32),
        name="mem_attention",
    )(x3, g, wq, qn, kv3, wo)


def _router_body(x_ref, g_ref, w_ref, h_ref, a_ref):
    hn = _rms_rows(x_ref[...], g_ref[...])
    h_ref[...] = hn.astype(BF16)
    h_hi, h_lo = _split2(hn)
    w_hi, w_lo = _split2(w_ref[...])
    logits = _dot(h_hi, w_hi) + _dot(h_hi, w_lo) + _dot(h_lo, w_hi)
    m = jnp.max(logits, axis=-1, keepdims=True)
    e = jnp.exp(logits - m)
    a_ref[...] = e / jnp.sum(e, axis=-1, keepdims=True)


def _router(x2, g, w_router, t=512):
    n = x2.shape[0]
    return pl.pallas_call(
        _router_body,
        grid=(n // t,),
        in_specs=[pl.BlockSpec((t, D_MODEL), lambda i: (i, 0)), pl.BlockSpec((1, D_MODEL), lambda i: (0, 0)),
                  pl.BlockSpec((D_MODEL, N_EXPERTS), lambda i: (0, 0))],
        out_specs=[pl.BlockSpec((t, D_MODEL), lambda i: (i, 0)), pl.BlockSpec((t, N_EXPERTS), lambda i: (i, 0))],
        out_shape=[jax.ShapeDtypeStruct((n, D_MODEL), BF16), jax.ShapeDtypeStruct((n, N_EXPERTS), F32)],
        compiler_params=_cparams(("parallel",), 32),
        name="router",
    )(x2, g, w_router)


def _thresh_body(a_ref, thr_ref, need_ref, *, cap):
    bits = pltpu.bitcast(a_ref[...], I32)

    def step(b, thr):
        cand = thr | (1 << (30 - b))
        cnt = jnp.sum((bits >= cand).astype(I32), axis=1, keepdims=True)
        return jnp.where(cnt >= cap, cand, thr)

    thr = lax.fori_loop(0, 31, step, jnp.zeros((N_EXPERTS, 1), I32))
    thr_ref[...] = thr
    need_ref[...] = cap - jnp.sum((bits > thr).astype(I32), axis=1, keepdims=True)


def _thresh(aff_t, cap):
    n = aff_t.shape[1]
    return pl.pallas_call(
        functools.partial(_thresh_body, cap=cap),
        grid=(1,),
        in_specs=[pl.BlockSpec((N_EXPERTS, n), lambda i: (0, 0))],
        out_specs=[pl.BlockSpec((N_EXPERTS, 1), lambda i: (0, 0))] * 2,
        out_shape=[jax.ShapeDtypeStruct((N_EXPERTS, 1), I32)] * 2,
        compiler_params=_cparams(("arbitrary",), 32),
        name="topk_threshold",
    )(aff_t)


def _slots_body(a_ref, thr_ref, need_ref, pos_ref, eq_off, sel_off):
    @pl.when(pl.program_id(0) == 0)
    def _():
        eq_off[...] = jnp.zeros_like(eq_off)
        sel_off[...] = jnp.zeros_like(sel_off)

    bits = pltpu.bitcast(a_ref[...], I32)
    w = bits.shape[1]
    thr = thr_ref[...]
    before = (lax.broadcasted_iota(I32, (w, w), 0) < lax.broadcasted_iota(I32, (w, w), 1)).astype(BF16)
    eq = bits == thr
    eq_f = eq.astype(F32)
    eq_rank = _dot(eq_f.astype(BF16), before) + eq_off[...]
    sel = (bits > thr) | (eq & (eq_rank < need_ref[...].astype(F32)))
    sel_f = sel.astype(F32)
    slot = _dot(sel_f.astype(BF16), before) + sel_off[...]
    pos_ref[...] = jnp.where(sel, slot, -1.0).astype(I32)
    eq_off[...] += jnp.sum(eq_f, axis=1, keepdims=True)
    sel_off[...] += jnp.sum(sel_f, axis=1, keepdims=True)


def _slots(aff_t, thr, need, w=TOK_BLK):
    n = aff_t.shape[1]
    col = pl.BlockSpec((N_EXPERTS, 1), lambda i: (0, 0))
    return pl.pallas_call(
        _slots_body,
        grid=(n // w,),
        in_specs=[pl.BlockSpec((N_EXPERTS, w), lambda i: (0, i)), col, col],
        out_specs=pl.BlockSpec((N_EXPERTS, w), lambda i: (0, i)),
        out_shape=jax.ShapeDtypeStruct((N_EXPERTS, n), I32),
        scratch_shapes=[pltpu.VMEM((N_EXPERTS, 1), F32)] * 2,
        compiler_params=_cparams(("arbitrary",), 32),
        name="topk_slots",
    )(aff_t, thr, need)


def _gather_body(cum_ref, h_ref, pos_ref, o_ref, *, cap):
    e = pl.program_id(0)
    sb = pl.program_id(1)
    nsub = SUPER // TOK_BLK

    @pl.when(sb == 0)
    def _():
        o_ref[...] = jnp.zeros_like(o_ref)

    slot_iota = lax.broadcasted_iota(I32, (SLOT_BLK, TOK_BLK), 0)
    for u in range(nsub):
        tb = sb * nsub + u
        lo = cum_ref[e, tb]
        hi = cum_ref[e, tb + 1]

        @pl.when(hi > lo)
        def _():
            hblk = h_ref[pl.ds(u * TOK_BLK, TOK_BLK), :]
            prow = pos_ref[:, pl.ds(u * TOK_BLK, TOK_BLK)]

            def one(j, carry):
                onehot = (slot_iota == (prow - j * SLOT_BLK)).astype(BF16)
                rows = pl.ds(pl.multiple_of(j * SLOT_BLK, SLOT_BLK), SLOT_BLK)
                o_ref[rows, :] += _dot(onehot, hblk).astype(BF16)
                return carry

            lax.fori_loop(lo // SLOT_BLK, (hi - 1) // SLOT_BLK + 1, one, 0)


def _gather(cum, hn, pos3, cap):
    n = hn.shape[0]
    return pl.pallas_call(
        functools.partial(_gather_body, cap=cap),
        grid_spec=pltpu.PrefetchScalarGridSpec(
            num_scalar_prefetch=1,
            grid=(N_EXPERTS, n // SUPER),
            in_specs=[pl.BlockSpec((SUPER, D_MODEL), lambda e, sb, cum: (sb, 0)),
                      pl.BlockSpec((None, 1, SUPER), lambda e, sb, cum: (e, 0, sb))],
            out_specs=pl.BlockSpec((None, cap, D_MODEL), lambda e, sb, cum: (e, 0, 0)),
        ),
        out_shape=jax.ShapeDtypeStruct((N_EXPERTS, cap, D_MODEL), BF16),
        compiler_params=_cparams(("parallel", "arbitrary"), 48),
        name="expert_gather",
    )(cum, hn, pos3)


def _ffn_body(x_ref, wg_ref, wu_ref, wd_ref, o_ref, acc_ref):
    f = pl.program_id(2)
    x = x_ref[...]
    gate = _dot(x, wg_ref[...])
    up = _dot(x, wu_ref[...])
    part = _dot((gate * _sigmoid(gate) * up).astype(BF16), wd_ref[...])

    @pl.when(f == 0)
    def _():
        acc_ref[...] = part

    @pl.when(f != 0)
    def _():
        acc_ref[...] += part

    @pl.when(f == pl.num_programs(2) - 1)
    def _():
        o_ref[...] = acc_ref[...].astype(BF16)


def _ffn(xe, wg, wu, wd, tm=1024, tf=D_FF_PAD // 2):
    e, cap, _ = xe.shape
    tm = min(tm, cap)
    return pl.pallas_call(
        _ffn_body,
        grid=(e, cap // tm, D_FF_PAD // tf),
        in_specs=[pl.BlockSpec((None, tm, D_MODEL), lambda ei, i, f: (ei, i, 0)),
                  pl.BlockSpec((None, D_MODEL, tf), lambda ei, i, f: (ei, 0, f)),
                  pl.BlockSpec((None, D_MODEL, tf), lambda ei, i, f: (ei, 0, f)),
                  pl.BlockSpec((None, tf, D_MODEL), lambda ei, i, f: (ei, f, 0))],
        out_specs=pl.BlockSpec((None, tm, D_MODEL), lambda ei, i, f: (ei, i, 0)),
        out_shape=jax.ShapeDtypeStruct((e, cap, D_MODEL), BF16),
        scratch_shapes=[pltpu.VMEM((tm, D_MODEL), F32)],
        compiler_params=_cparams(("parallel", "parallel", "arbitrary"), 56),
        name="expert_ffn",
    )(xe, wg, wu, wd)


def _combine_body(cum_ref, x_ref, aff_ref, pos_ref, y_ref, o_ref):
    sb = pl.program_id(0)
    e = pl.program_id(1)
    nsub = SUPER // TOK_BLK

    @pl.when(e == 0)
    def _():
        o_ref[...] = x_ref[...]

    lane = lax.broadcasted_iota(I32, (TOK_BLK, N_EXPERTS), 1)
    slot_iota = lax.broadcasted_iota(I32, (TOK_BLK, SLOT_BLK), 1)
    for u in range(nsub):
        tb = sb * nsub + u
        lo = cum_ref[e, tb]
        hi = cum_ref[e, tb + 1]

        @pl.when(hi > lo)
        def _():
            rows = pl.ds(u * TOK_BLK, TOK_BLK)
            pcol = jnp.sum(jnp.where(lane == e, pos_ref[rows, :], 0), axis=1, keepdims=True)
            gcol = jnp.sum(jnp.where(lane == e, aff_ref[rows, :], 0.0), axis=1, keepdims=True)

            def one(j, carry):
                onehot = (slot_iota == (pcol - j * SLOT_BLK)).astype(BF16)
                yblk = y_ref[pl.ds(pl.multiple_of(j * SLOT_BLK, SLOT_BLK), SLOT_BLK), :]
                o_ref[rows, :] += gcol * _dot(onehot, yblk)
                return carry

            lax.fori_loop(lo // SLOT_BLK, (hi - 1) // SLOT_BLK + 1, one, 0)


def _combine(cum, x2, aff, pos_t, ye):
    n = x2.shape[0]
    cap = ye.shape[1]
    return pl.pallas_call(
        _combine_body,
        grid_spec=pltpu.PrefetchScalarGridSpec(
            num_scalar_prefetch=1,
            grid=(n // SUPER, N_EXPERTS),
            in_specs=[pl.BlockSpec((SUPER, D_MODEL), lambda sb, e, cum: (sb, 0)),
                      pl.BlockSpec((SUPER, N_EXPERTS), lambda sb, e, cum: (sb, 0)),
                      pl.BlockSpec((SUPER, N_EXPERTS), lambda sb, e, cum: (sb, 0)),
                      pl.BlockSpec((None, cap, D_MODEL), lambda sb, e, cum: (e, 0, 0))],
            out_specs=pl.BlockSpec((SUPER, D_MODEL), lambda sb, e, cum: (sb, 0)),
        ),
        out_shape=jax.ShapeDtypeStruct((n, D_MODEL), F32),
        compiler_params=_cparams(("parallel", "arbitrary"), 56),
        name="expert_combine",
    )(cum, x2, aff, pos_t, ye)


def _expert_choice(x2, g, w_router, wg, wu, wd):
    n = x2.shape[0]
    cap = EC_CAPACITY * n // N_EXPERTS
    hn, aff = _router(x2, g, w_router)
    aff_t = aff.T
    thr, need = _thresh(aff_t, cap)
    pos = _slots(aff_t, thr, need)
    cnt = jnp.sum((pos >= 0).reshape(N_EXPERTS, n // TOK_BLK, TOK_BLK), axis=-1, dtype=I32)
    cum = jnp.concatenate([jnp.zeros((N_EXPERTS, 1), I32), jnp.cumsum(cnt, axis=1, dtype=I32)], axis=1)
    xe = _gather(cum, hn, pos.reshape(N_EXPERTS, 1, n), cap)
    ye = _ffn(xe, wg, wu, wd)
    return _combine(cum, x2, aff, pos.T, ye)


def _prep_layer(l, p, lb):
    row = lambda a: a.astype(F32).reshape(1, -1)
    qkg = jnp.concatenate([jnp.tile(p["na_q_norm"][l] * (NA_HD ** -0.5), NA_HEADS),
                           jnp.tile(p["na_k_norm"][l], NA_HEADS)]).reshape(1, 2 * NA_WIDTH)
    idx = jnp.arange(MXU_DIM) // NA_HD
    pad = D_FF_PAD - D_FF
    wgu = p["w_gate_up"][l]
    return dict(
        norm_mix=row(p["norm_mix"][l]), w_in=p["w_in"][l].astype(BF16),
        grp=(idx[:, None] == idx[None, :]).astype(BF16), qkg=qkg.astype(F32),
        bias_tbl=_na_bias_table(p["na_rel_bias"][l]),
        lb4=lb[:, l].reshape(2, HG_HEADS, 1, HG_DK),
        conv_w=p["conv_w"][l].astype(F32), conv_b=row(p["conv_b"][l]),
        ln_g=row(p["conv_ln_g"][l]), ln_b=row(p["conv_ln_b"][l]),
        og=jnp.tile(p["hg_out_norm"][l], HG_HEADS).reshape(1, -1).astype(F32),
        wna=p["w_na_br"][l].astype(BF16), whg=p["w_hg_br"][l].astype(BF16), wcv=p["w_cv_br"][l].astype(BF16),
        wout=p["w_out"][l].astype(BF16),
        norm_mem=row(p["norm_mem"][l]), mem_kv_norm=row(p["mem_kv_norm"][l]),
        wq=p["wq_mem"][l].astype(BF16), wkv=p["wkv_mem"][l].astype(BF16),
        qn=row(p["mem_q_norm"][l]), kn=row(p["mem_k_norm"][l]), wo=p["wo_mem"][l].astype(BF16),
        norm_ffn=row(p["norm_ffn"][l]), w_router=p["w_router"][l].astype(F32),
        wg=jnp.pad(wgu[:, :, :D_FF], ((0, 0), (0, 0), (0, pad))).astype(BF16),
        wu=jnp.pad(wgu[:, :, D_FF:], ((0, 0), (0, 0), (0, pad))).astype(BF16),
        wd=jnp.pad(p["w_down"][l], ((0, 0), (0, pad), (0, 0))).astype(BF16),
    )


def _layer(x3, mem3, w):
    b, l, _ = x3.shape
    n = b * l
    x2 = x3.reshape(n, D_MODEL)
    proj2 = _inproj(x2, w["norm_mix"], w["w_in"], w["grp"], w["qkg"])
    proj3 = proj2.reshape(b, l, D_IN)
    na = _na(proj3, w["bias_tbl"])
    o_f, o_b = _hgrn(proj3, w["lb4"])
    cv = _conv(proj3, w["conv_w"], w["conv_b"], w["ln_g"], w["ln_b"])
    x2 = _merge(na.reshape(n, -1), o_f.reshape(n, -1), o_b.reshape(n, -1), proj2, cv.reshape(n, -1), x2,
                w["og"], w["wna"], w["whg"], w["wcv"], w["wout"])
    kv = _memkv(mem3.reshape(-1, D_MODEL), w["mem_kv_norm"], w["wkv"], w["kn"])
    x3 = _memattn(x2.reshape(b, l, D_MODEL), w["norm_mem"], w["wq"], w["qn"],
                  kv.reshape(b, -1, 2 * MEM_WIDTH), w["wo"])
    x2 = _expert_choice(x3.reshape(n, D_MODEL), w["norm_ffn"], w["w_router"], w["wg"], w["wu"], w["wd"])
    return x2.reshape(b, l, D_MODEL)


def kernel(x_prompt, x_sample, mem_prompt, mem_sample, norm_mix, w_in, na_q_norm, na_k_norm, na_rel_bias, w_na_br, hg_lb, hg_out_norm, w_hg_br, conv_w, conv_b, conv_ln_g, conv_ln_b, w_cv_br, w_out, norm_mem, mem_kv_norm, wq_mem, wkv_mem, mem_q_norm, mem_k_norm, wo_mem, norm_ffn, w_router, w_gate_up, w_down):
    p = dict(norm_mix=norm_mix, w_in=w_in, na_q_norm=na_q_norm, na_k_norm=na_k_norm, na_rel_bias=na_rel_bias,
             w_na_br=w_na_br, hg_out_norm=hg_out_norm, w_hg_br=w_hg_br, conv_w=conv_w, conv_b=conv_b,
             conv_ln_g=conv_ln_g, conv_ln_b=conv_ln_b, w_cv_br=w_cv_br, w_out=w_out, norm_mem=norm_mem,
             mem_kv_norm=mem_kv_norm, wq_mem=wq_mem, wkv_mem=wkv_mem, mem_q_norm=mem_q_norm,
             mem_k_norm=mem_k_norm, wo_mem=wo_mem, norm_ffn=norm_ffn, w_router=w_router, w_gate_up=w_gate_up,
             w_down=w_down)
    depth = w_in.shape[0]
    sm = jax.nn.softmax(hg_lb.astype(F32), axis=1)
    lb = jnp.cumsum(sm, axis=1) - sm[:, :1]
    xs = [x_prompt, x_sample]
    mems = [mem_prompt, mem_sample]
    for l in range(depth):
        w = _prep_layer(l, p, lb)
        xs = [_layer(x, m, w) for x, m in zip(xs, mems)]
    return tuple(xs)
```

```python
import functools

import jax
import jax.numpy as jnp
import numpy as np
from jax import lax
from jax.experimental import pallas as pl
from jax.experimental.pallas import tpu as pltpu

F32 = jnp.float32
BF16 = jnp.bfloat16
I32 = jnp.int32

D_MODEL = 1024
GRID_W = 64
NA_HEADS = 8
NA_HD = 64
NA_WIDTH = NA_HEADS * NA_HD
WIN_R = 8
WIN_C = 16
HG_HEADS = 4
HG_DK = 128
HG_DV = 128
HG_KW = HG_HEADS * HG_DK
F_MIN = 1e-20
D_CONV = 512
CONV_K = 31
MEM_HEADS = 4
MEM_HD = 128
MEM_WIDTH = MEM_HEADS * MEM_HD
N_EXPERTS = 16
EC_CAPACITY = 2
D_FF = 2752
EPS = 1e-6
D_IN = 8192

COL_NA_Q, COL_NA_K, COL_NA_V = 0, 1, 2
COL_HG_Q, COL_HG_FF, COL_HG_FB, COL_HG_V, COL_HG_G = 3, 4, 5, 6, 7
COL_CV_A, COL_CV_B = 8, 9
COL_G_NA, COL_G_HG, COL_G_CV = 5, 6, 7

LANES = 128
SUBLANES = 8
MXU_DIM = 256
D_FF_PAD = 2816
NEG_BIG = -1e30
LOG2E = 1.4426950408889634

HG_CHUNK = 256
TOK_BLK = 256
ROW_TILE = D_MODEL // LANES
COMB_WIN = 64


def _cparams(sem, vmem_mb):
    return pltpu.CompilerParams(dimension_semantics=sem, vmem_limit_bytes=vmem_mb << 20)


def _sigmoid_pair(x):
    t = 0.5 * jnp.tanh(0.5 * x)
    return 0.5 + t, 0.5 - t


def _sigmoid(x):
    return 0.5 + 0.5 * jnp.tanh(0.5 * x)


def _neg_abs(x):
    return lax.bitcast_convert_type(lax.bitcast_convert_type(x, I32) | jnp.int32(-2 ** 31), F32)


def _split2(a):
    hi = a.astype(BF16)
    lo = (a - hi.astype(F32)).astype(BF16)
    return hi, lo


def _dot(a, b):
    return jnp.dot(a, b, preferred_element_type=F32)


def _dot_nt(a, b):
    return lax.dot_general(a, b, (((1,), (1,)), ((), ())), preferred_element_type=F32)


def _dot_tn(a, b):
    return lax.dot_general(a, b, (((0,), (0,)), ((), ())), preferred_element_type=F32)


def _rms_rows(x, g):
    return x * lax.rsqrt(jnp.mean(x * x, axis=-1, keepdims=True) + EPS) * g


def _inproj_body(x_ref, g_ref, w_ref, grp_ref, qkg_ref, o_ref, h_ref):
    j = pl.program_id(1)

    @pl.when(j == 0)
    def _():
        h_ref[...] = _rms_rows(x_ref[...], g_ref[...]).astype(BF16)

    acc = _dot(h_ref[...], w_ref[...])

    @pl.when(j == 0)
    def _():
        sq = (acc * acc).astype(BF16)
        w = grp_ref.shape[0]
        ss = jnp.concatenate([_dot(sq[:, c * w:(c + 1) * w], grp_ref[...]) for c in range(sq.shape[1] // w)], axis=1)
        o_ref[...] = (acc * lax.rsqrt(ss * (1.0 / NA_HD) + EPS) * qkg_ref[...]).astype(BF16)

    @pl.when(j != 0)
    def _():
        o_ref[...] = acc.astype(BF16)


def _inproj(x2, g, w_bf, grp, qkg, tm=1024, tn=1024):
    n = x2.shape[0]
    return pl.pallas_call(
        _inproj_body,
        grid=(n // tm, D_IN // tn),
        in_specs=[
            pl.BlockSpec((tm, D_MODEL), lambda i, j: (i, 0)),
            pl.BlockSpec((1, D_MODEL), lambda i, j: (0, 0)),
            pl.BlockSpec((D_MODEL, tn), lambda i, j: (0, j)),
            pl.BlockSpec((MXU_DIM, MXU_DIM), lambda i, j: (0, 0)),
            pl.BlockSpec((1, tn), lambda i, j: (0, 0)),
        ],
        out_specs=pl.BlockSpec((tm, tn), lambda i, j: (i, j)),
        out_shape=jax.ShapeDtypeStruct((n, D_IN), BF16),
        scratch_shapes=[pltpu.VMEM((tm, D_MODEL), BF16)],
        compiler_params=_cparams(("parallel", "arbitrary"), 48),
        name="in_proj",
    )(x2, g, w_bf, grp, qkg)


NA_QROWS = 4
NA_KROWS = NA_QROWS + WIN_R


def _na_body(q_ref, k_ref, v_ref, bias_ref, o_ref):
    q = q_ref[...]
    k = k_ref[0]
    v = v_ref[0]
    outs = []
    for h in range(NA_HEADS):
        sl = slice(h * NA_HD, (h + 1) * NA_HD)
        s = _dot_nt(q[:, sl], k[:, sl]) + bias_ref[h]
        m = jnp.max(s, axis=-1, keepdims=True)
        p = jnp.exp(s - m)
        l = jnp.sum(p, axis=-1, keepdims=True)
        outs.append(_dot(p.astype(BF16), v[:, sl]) / l)
    o_ref[...] = jnp.concatenate(outs, axis=1).astype(BF16)


def _na(proj3, bias_tbl):
    b, l, _ = proj3.shape
    rows = l // GRID_W
    assert rows % NA_QROWS == 0 and rows >= 2 * WIN_R
    nblk = rows // NA_QROWS

    def key_row0(g):
        return jnp.clip(NA_QROWS * g - WIN_R // 2, 0, rows - NA_KROWS)

    def variant(g):
        return jnp.where(g == 0, 0, jnp.where(g == nblk - 1, 2, 1))

    qblk = (None, NA_QROWS * GRID_W, NA_WIDTH)
    kblk = (pl.Element(1), pl.Element(NA_KROWS * GRID_W), pl.Element(NA_WIDTH))
    return pl.pallas_call(
        _na_body,
        grid=(b, nblk),
        in_specs=[pl.BlockSpec(qblk, lambda bi, g: (bi, g, COL_NA_Q)),
                  pl.BlockSpec(kblk, lambda bi, g: (bi, key_row0(g) * GRID_W, COL_NA_K * NA_WIDTH)),
                  pl.BlockSpec(kblk, lambda bi, g: (bi, key_row0(g) * GRID_W, COL_NA_V * NA_WIDTH)),
                  pl.BlockSpec((None, NA_HEADS, NA_QROWS * GRID_W, NA_KROWS * GRID_W),
                               lambda bi, g: (variant(g), 0, 0, 0))],
        out_specs=pl.BlockSpec(qblk, lambda bi, g: (bi, g, 0)),
        out_shape=jax.ShapeDtypeStruct((b, l, NA_WIDTH), BF16),
        compiler_params=_cparams(("parallel", "arbitrary"), 48),
        name="na_attention",
    )(proj3, proj3, proj3, bias_tbl)


def _na_bias_table(rel_bias):
    c = np.arange(GRID_W)
    c0 = np.clip(c - WIN_C // 2, 0, GRID_W - WIN_C)
    kc = np.arange(GRID_W)
    col_ok = (kc[None, :] >= c0[:, None]) & (kc[None, :] < c0[:, None] + WIN_C)
    dcol = kc[None, :] - c[:, None] + WIN_C - 1
    pick = (dcol[:, :, None] == np.arange(2 * WIN_C - 1)[None, None, :]) & col_ok[:, :, None]
    toe = jnp.einsum('hrd,ckd->hrck', rel_bias.astype(F32), jnp.asarray(pick, F32),
                     precision=lax.Precision.HIGHEST)
    toe = jnp.where(jnp.asarray(col_ok)[None, None], toe, NEG_BIG)
    pad = NA_KROWS
    toe = jnp.pad(toe, ((0, 0), (pad, pad), (0, 0), (0, 0)), constant_values=NEG_BIG)
    i = np.arange(NA_KROWS)
    variants = []
    for var in range(3):
        per_row = []
        for rho in range(NA_QROWS):
            off = {0: rho, 1: rho + WIN_R // 2, 2: rho + WIN_R}[var]
            lo = {0: 0, 1: rho, 2: NA_QROWS}[var]
            row_ok = (i >= lo) & (i < lo + WIN_R)
            dr0 = -off + WIN_R - 1 + pad
            t = toe[:, dr0:dr0 + NA_KROWS]
            t = jnp.where(jnp.asarray(row_ok)[None, :, None, None], t, NEG_BIG)
            per_row.append(t.transpose(0, 2, 1, 3).reshape(NA_HEADS, GRID_W, NA_KROWS * GRID_W))
        variants.append(jnp.concatenate(per_row, axis=1))
    return jnp.stack(variants)


def _hgrn_chunk(hq, hf, v, lb, st_ref, rev):
    c = hq.shape[0]
    hq = hq.astype(F32)
    q = hq * _sigmoid(hq) * (HG_DK ** -0.5)
    s_pos, s_neg = _sigmoid_pair(hf.astype(F32))
    f = lb + (1.0 - lb) * s_pos
    logf = jnp.log(jnp.maximum(f, F_MIN))
    k = (1.0 - lb) * s_neg

    hb = c // 2
    t_col = lax.broadcasted_iota(I32, (c, 1), 0)
    t_mat = lax.broadcasted_iota(I32, (c, c), 0)
    s_mat = lax.broadcasted_iota(I32, (c, c), 1)
    tri = (s_mat >= t_mat) if rev else (s_mat <= t_mat)
    hi, lo = _split2(logf)
    cs = _dot(tri.astype(BF16), jnp.concatenate([hi, lo], axis=1))
    a = (cs[:, :HG_DK] + cs[:, HG_DK:]) * LOG2E

    st = st_ref[...]
    o_state = _dot_nt((q * jnp.exp2(a)).astype(BF16), st.astype(BF16))

    tl = lax.broadcasted_iota(I32, (hb, hb), 0)
    sl = lax.broadcasted_iota(I32, (hb, hb), 1)
    x = tl ^ sl
    keep = (tl <= sl) if rev else (tl >= sl)
    dg = jnp.sum(q * k, axis=1, keepdims=True)
    att = [jnp.where(x == 0, dg[b * hb:(b + 1) * hb], 0.0) for b in range(2)]
    a8 = a.reshape(c // 8, 8, HG_DK)
    sub = lax.broadcasted_iota(I32, (1, 8, 1), 1)
    m = 1
    while m < hb:
        if m >= 4:
            ar = a.reshape(c // (2 * m), 2 * m, HG_DK)
            row = m if rev else m - 1
            ab = jnp.broadcast_to(ar[:, row:row + 1, :], ar.shape).reshape(c, HG_DK)
        else:
            off = m if rev else m - 1
            rows = [a8[:, p + off:p + off + 1, :] for p in range(0, 8, 2 * m)]
            ab = rows[-1]
            for idx in range(len(rows) - 2, -1, -1):
                ab = jnp.where(sub < (idx + 1) * 2 * m, rows[idx], ab)
            ab = jnp.broadcast_to(ab, a8.shape).reshape(c, HG_DK)
        e = jnp.exp2(_neg_abs(a - ab))
        upper = (t_col & m) != 0
        q_half = jnp.logical_not(upper) if rev else upper
        z = (jnp.where(q_half, q, k) * e).astype(BF16)
        for b in range(2):
            zb = z[b * hb:(b + 1) * hb]
            att[b] = jnp.where(x >= m, _dot_nt(zb, zb), att[b])
        m *= 2
    att = [jnp.where(keep, t, 0.0).astype(BF16) for t in att]

    qs, ks = (slice(0, hb), slice(hb, c)) if rev else (slice(hb, c), slice(0, hb))
    ab = a[hb:hb + 1] if rev else a[hb - 1:hb]
    qe = (q[qs] * jnp.exp2(_neg_abs(a[qs] - ab))).astype(BF16)
    ke = (k[ks] * jnp.exp2(_neg_abs(a[ks] - ab))).astype(BF16)
    cross = _dot(_dot_nt(qe, ke).astype(BF16), v[ks])
    o0 = _dot(att[0], v[:hb])
    o1 = _dot(att[1], v[hb:])
    o = o_state + (jnp.concatenate([o0 + cross, o1], axis=0) if rev else jnp.concatenate([o0, o1 + cross], axis=0))

    a_end = a[0:1, :] if rev else a[c - 1:c, :]
    k_end = (k * jnp.exp2(a_end - a)).astype(BF16)
    st_ref[...] = st * jnp.exp2(a_end) + _dot_tn(v, k_end)
    return o


def _hgrn_body(qf_ref, ff_ref, vf_ref, qb_ref, fb_ref, vb_ref, lb_ref, of_ref, ob_ref, st_ref):
    @pl.when(pl.program_id(2) == 0)
    def _():
        st_ref[...] = jnp.zeros_like(st_ref)

    of_ref[...] = _hgrn_chunk(qf_ref[...], ff_ref[...], vf_ref[...], lb_ref[0, 0], st_ref.at[0], False)
    ob_ref[...] = _hgrn_chunk(qb_ref[...], fb_ref[...], vb_ref[...], lb_ref[1, 0], st_ref.at[1], True)


def _hgrn(proj3, lb4, chunk=HG_CHUNK):
    b, l, _ = proj3.shape
    n = l // chunk
    hpb = 512 // HG_DK

    def spec(col, back):
        if back:
            return pl.BlockSpec((None, chunk, HG_DK), lambda bi, h, i: (bi, n - 1 - i, col * hpb + h))
        return pl.BlockSpec((None, chunk, HG_DK), lambda bi, h, i: (bi, i, col * hpb + h))

    out_f = pl.BlockSpec((None, chunk, HG_DV), lambda bi, h, i: (bi, i, h))
    out_b = pl.BlockSpec((None, chunk, HG_DV), lambda bi, h, i: (bi, n - 1 - i, h))
    return pl.pallas_call(
        _hgrn_body,
        grid=(b, HG_HEADS, n),
        in_specs=[spec(COL_HG_Q, False), spec(COL_HG_FF, False), spec(COL_HG_V, False),
                  spec(COL_HG_Q, True), spec(COL_HG_FB, True), spec(COL_HG_V, True),
                  pl.BlockSpec((2, None, 1, HG_DK), lambda bi, h, i: (0, h, 0, 0))],
        out_specs=[out_f, out_b],
        out_shape=[jax.ShapeDtypeStruct((b, l, HG_HEADS * HG_DV), F32)] * 2,
        scratch_shapes=[pltpu.VMEM((2, HG_DV, HG_DK), F32)],
        compiler_params=_cparams(("parallel", "parallel", "arbitrary"), 32),
        name="hgrn2",
    )(*([proj3] * 6), lb4)


CONV_HALO = 16


def _conv_body(a_ref, b_ref, ap_ref, bp_ref, an_ref, bn_ref, w_ref, cb_ref, g_ref, be_ref, o_ref, u_ref, sh_ref):
    i = pl.program_id(1)
    n = pl.num_programs(1)
    t = a_ref.shape[0]

    def glu(a, b):
        return a[...].astype(F32) * _sigmoid(b[...].astype(F32))

    u_ref[pl.ds(0, CONV_HALO), :] = jnp.where(i > 0, glu(ap_ref, bp_ref), 0.0)
    u_ref[pl.ds(CONV_HALO, t), :] = glu(a_ref, b_ref)
    u_ref[pl.ds(CONV_HALO + t, CONV_HALO), :] = jnp.where(i < n - 1, glu(an_ref, bn_ref), 0.0)
    span = t + 2 * CONV_HALO - SUBLANES
    for r in range(1, SUBLANES):
        sh_ref[r - 1] = u_ref[pl.ds(r, span), :]
    base = CONV_HALO - CONV_K // 2
    acc = jnp.zeros((t, D_CONV), F32)
    for kk in range(CONV_K):
        off = base + kk
        r = off % SUBLANES
        win = u_ref[pl.ds(off, t), :] if r == 0 else sh_ref[r - 1, pl.ds(off - r, t), :]
        acc = acc + w_ref[kk:kk + 1, :] * win
    y = acc + cb_ref[...]
    mu = jnp.mean(y, axis=-1, keepdims=True)
    yc = y - mu
    var = jnp.mean(yc * yc, axis=-1, keepdims=True)
    yn = yc * lax.rsqrt(var + EPS) * g_ref[...] + be_ref[...]
    o_ref[...] = (yn * _sigmoid(yn)).astype(BF16)


def _conv(proj3, conv_w, conv_b, ln_g, ln_b, t=512):
    b, l, _ = proj3.shape
    nt = l // t
    hb = t // CONV_HALO
    nh = l // CONV_HALO
    main = lambda col: pl.BlockSpec((None, t, D_CONV), lambda bi, i: (bi, i, col))
    prev = lambda col: pl.BlockSpec((None, CONV_HALO, D_CONV), lambda bi, i: (bi, jnp.maximum(i * hb - 1, 0), col))
    nxt = lambda col: pl.BlockSpec((None, CONV_HALO, D_CONV), lambda bi, i: (bi, jnp.minimum((i + 1) * hb, nh - 1), col))
    vec = pl.BlockSpec((1, D_CONV), lambda bi, i: (0, 0))
    return pl.pallas_call(
        _conv_body,
        grid=(b, nt),
        in_specs=[main(COL_CV_A), main(COL_CV_B), prev(COL_CV_A), prev(COL_CV_B), nxt(COL_CV_A), nxt(COL_CV_B),
                  pl.BlockSpec((CONV_K, D_CONV), lambda bi, i: (0, 0)), vec, vec, vec],
        out_specs=pl.BlockSpec((None, t, D_CONV), lambda bi, i: (bi, i, 0)),
        out_shape=jax.ShapeDtypeStruct((b, l, D_CONV), BF16),
        scratch_shapes=[pltpu.VMEM((t + 2 * CONV_HALO, D_CONV), F32),
                        pltpu.VMEM((SUBLANES - 1, t + 2 * CONV_HALO - SUBLANES, D_CONV), F32)],
        compiler_params=_cparams(("parallel", "arbitrary"), 32),
        name="conv_module",
    )(*([proj3] * 6), conv_w, conv_b, ln_g, ln_b)


def _merge_body(na_ref, of_ref, ob_ref, hg_ref, cv_ref, gna_ref, ghg_ref, gcv_ref, x_ref,
                og_ref, wna_ref, whg_ref, wcv_ref, wout_ref, o_ref):
    o = of_ref[...] + ob_ref[...]
    parts = []
    for h in range(HG_HEADS):
        oh = o[:, h * HG_DV:(h + 1) * HG_DV]
        parts.append(oh * lax.rsqrt(jnp.mean(oh * oh, axis=-1, keepdims=True) + EPS))
    gg = hg_ref[...].astype(F32)
    hg = jnp.concatenate(parts, axis=1) * og_ref[...] * (gg * _sigmoid(gg))
    merged = (_sigmoid(gna_ref[...].astype(F32)) * _dot(na_ref[...], wna_ref[...])
              + _sigmoid(ghg_ref[...].astype(F32)) * _dot(hg.astype(BF16), whg_ref[...])
              + _sigmoid(gcv_ref[...].astype(F32)) * _dot(cv_ref[...], wcv_ref[...]))
    o_ref[...] = x_ref[...] + _dot(merged.astype(BF16), wout_ref[...])


def _merge(na2, of2, ob2, proj2, cv2, x2, og, wna, whg, wcv, wout, t=512):
    n = x2.shape[0]
    half = lambda col=0: pl.BlockSpec((t, 512), lambda i: (i, col))
    full = lambda col=0: pl.BlockSpec((t, D_MODEL), lambda i: (i, col))
    wspec = lambda r: pl.BlockSpec((r, D_MODEL), lambda i: (0, 0))
    return pl.pallas_call(
        _merge_body,
        grid=(n // t,),
        in_specs=[half(), half(), half(), half(COL_HG_G), half(), full(COL_G_NA), full(COL_G_HG), full(COL_G_CV),
                  full(), pl.BlockSpec((1, 512), lambda i: (0, 0)), wspec(512), wspec(512), wspec(512),
                  wspec(D_MODEL)],
        out_specs=full(),
        out_shape=jax.ShapeDtypeStruct((n, D_MODEL), F32),
        compiler_params=_cparams(("parallel",), 48),
        name="merge_out_proj",
    )(na2, of2, ob2, proj2, cv2, proj2, proj2, proj2, x2, og, wna, whg, wcv, wout)


def _memkv_body(m_ref, g_ref, w_ref, kn_ref, o_ref):
    kv = _dot(_rms_rows(m_ref[...], g_ref[...]).astype(BF16), w_ref[...])
    parts = []
    for h in range(MEM_HEADS):
        kh = kv[:, h * MEM_HD:(h + 1) * MEM_HD]
        parts.append(_rms_rows(kh, kn_ref[...]))
    parts.append(kv[:, MEM_WIDTH:])
    o_ref[...] = jnp.concatenate(parts, axis=1).astype(BF16)


def _memkv(mem2, g, wkv, kn, t=256):
    n = mem2.shape[0]
    return pl.pallas_call(
        _memkv_body,
        grid=(n // t,),
        in_specs=[pl.BlockSpec((t, D_MODEL), lambda i: (i, 0)), pl.BlockSpec((1, D_MODEL), lambda i: (0, 0)),
                  pl.BlockSpec((D_MODEL, 2 * MEM_WIDTH), lambda i: (0, 0)),
                  pl.BlockSpec((1, MEM_HD), lambda i: (0, 0))],
        out_specs=pl.BlockSpec((t, 2 * MEM_WIDTH), lambda i: (i, 0)),
        out_shape=jax.ShapeDtypeStruct((n, 2 * MEM_WIDTH), BF16),
        compiler_params=_cparams(("parallel",), 32),
        name="mem_kv",
    )(mem2, g, wkv, kn)


def _memattn_body(x_ref, g_ref, wq_ref, qn_ref, kv_ref, wo_ref, o_ref):
    x = x_ref[...]
    q = _dot(_rms_rows(x, g_ref[...]).astype(BF16), wq_ref[...])
    kv = kv_ref[...]
    outs = []
    for h in range(MEM_HEADS):
        sl = slice(h * MEM_HD, (h + 1) * MEM_HD)
        qh = _rms_rows(q[:, sl], qn_ref[...]) * (MEM_HD ** -0.5)
        s = _dot_nt(qh.astype(BF16), kv[:, sl])
        m = jnp.max(s, axis=-1, keepdims=True)
        p = jnp.exp(s - m)
        l = jnp.sum(p, axis=-1, keepdims=True)
        outs.append(_dot(p.astype(BF16), kv[:, MEM_WIDTH + h * MEM_HD:MEM_WIDTH + (h + 1) * MEM_HD]) / l)
    o_ref[...] = x + _dot(jnp.concatenate(outs, axis=1).astype(BF16), wo_ref[...])


def _memattn(x3, g, wq, qn, kv3, wo, t=512):
    b, l, _ = x3.shape
    m = kv3.shape[1]
    return pl.pallas_call(
        _memattn_body,
        grid=(b, l // t),
        in_specs=[pl.BlockSpec((None, t, D_MODEL), lambda bi, i: (bi, i, 0)),
                  pl.BlockSpec((1, D_MODEL), lambda bi, i: (0, 0)),
                  pl.BlockSpec((D_MODEL, MEM_WIDTH), lambda bi, i: (0, 0)),
                  pl.BlockSpec((1, MEM_HD), lambda bi, i: (0, 0)),
                  pl.BlockSpec((None, m, 2 * MEM_WIDTH), lambda bi, i: (bi, 0, 0)),
                  pl.BlockSpec((MEM_WIDTH, D_MODEL), lambda bi, i: (0, 0))],
        out_specs=pl.BlockSpec((None, t, D_MODEL), lambda bi, i: (bi, i, 0)),
        out_shape=jax.ShapeDtypeStruct((b, l, D_MODEL), F32),
        compiler_params=_cparams(("parallel", "arbitrary"), 32),
        name="mem_attention",
    )(x3, g, wq, qn, kv3, wo)


def _router_body(x_ref, g_ref, w_ref, h_ref, a_ref):
    hn = _rms_rows(x_ref[...], g_ref[...])
    h_ref[...] = hn.astype(BF16)
    h_hi, h_lo = _split2(hn)
    w_hi, w_lo = _split2(w_ref[...])
    logits = _dot(h_hi, w_hi) + _dot(h_hi, w_lo) + _dot(h_lo, w_hi)
    m = jnp.max(logits, axis=-1, keepdims=True)
    e = jnp.exp(logits - m)
    a_ref[...] = e / jnp.sum(e, axis=-1, keepdims=True)


def _router(x2, g, w_router, t=512):
    n = x2.shape[0]
    return pl.pallas_call(
        _router_body,
        grid=(n // t,),
        in_specs=[pl.BlockSpec((t, D_MODEL), lambda i: (i, 0)), pl.BlockSpec((1, D_MODEL), lambda i: (0, 0)),
                  pl.BlockSpec((D_MODEL, N_EXPERTS), lambda i: (0, 0))],
        out_specs=[pl.BlockSpec((t, D_MODEL), lambda i: (i, 0)), pl.BlockSpec((t, N_EXPERTS), lambda i: (i, 0))],
        out_shape=[jax.ShapeDtypeStruct((n, D_MODEL), BF16), jax.ShapeDtypeStruct((n, N_EXPERTS), F32)],
        compiler_params=_cparams(("parallel",), 32),
        name="router",
    )(x2, g, w_router)


def _thresh_body(a_ref, thr_ref, need_ref, *, cap):
    bits = pltpu.bitcast(a_ref[...], I32)

    def step(b, thr):
        cand = thr | (1 << (30 - b))
        cnt = jnp.sum((bits >= cand).astype(I32), axis=1, keepdims=True)
        return jnp.where(cnt >= cap, cand, thr)

    thr = lax.fori_loop(0, 31, step, jnp.zeros((N_EXPERTS, 1), I32))
    thr_ref[...] = thr
    need_ref[...] = cap - jnp.sum((bits > thr).astype(I32), axis=1, keepdims=True)


def _thresh(aff_t, cap):
    n = aff_t.shape[1]
    return pl.pallas_call(
        functools.partial(_thresh_body, cap=cap),
        grid=(1,),
        in_specs=[pl.BlockSpec((N_EXPERTS, n), lambda i: (0, 0))],
        out_specs=[pl.BlockSpec((N_EXPERTS, 1), lambda i: (0, 0))] * 2,
        out_shape=[jax.ShapeDtypeStruct((N_EXPERTS, 1), I32)] * 2,
        compiler_params=_cparams(("arbitrary",), 32),
        name="topk_threshold",
    )(aff_t)


def _slots_body(a_ref, thr_ref, need_ref, pos_ref, eq_off, sel_off):
    @pl.when(pl.program_id(0) == 0)
    def _():
        eq_off[...] = jnp.zeros_like(eq_off)
        sel_off[...] = jnp.zeros_like(sel_off)

    bits = pltpu.bitcast(a_ref[...], I32)
    w = bits.shape[1]
    thr = thr_ref[...]
    before = (lax.broadcasted_iota(I32, (w, w), 0) < lax.broadcasted_iota(I32, (w, w), 1)).astype(BF16)
    eq = bits == thr
    eq_f = eq.astype(F32)
    eq_rank = _dot(eq_f.astype(BF16), before) + eq_off[...]
    sel = (bits > thr) | (eq & (eq_rank < need_ref[...].astype(F32)))
    sel_f = sel.astype(F32)
    slot = _dot(sel_f.astype(BF16), before) + sel_off[...]
    pos_ref[...] = jnp.where(sel, slot, -1.0).astype(I32)
    eq_off[...] += jnp.sum(eq_f, axis=1, keepdims=True)
    sel_off[...] += jnp.sum(sel_f, axis=1, keepdims=True)


def _slots(aff_t, thr, need, w=TOK_BLK):
    n = aff_t.shape[1]
    col = pl.BlockSpec((N_EXPERTS, 1), lambda i: (0, 0))
    return pl.pallas_call(
        _slots_body,
        grid=(n // w,),
        in_specs=[pl.BlockSpec((N_EXPERTS, w), lambda i: (0, i)), col, col],
        out_specs=pl.BlockSpec((N_EXPERTS, w), lambda i: (0, i)),
        out_shape=jax.ShapeDtypeStruct((N_EXPERTS, n), I32),
        scratch_shapes=[pltpu.VMEM((N_EXPERTS, 1), F32)] * 2,
        compiler_params=_cparams(("arbitrary",), 32),
        name="topk_slots",
    )(aff_t, thr, need)


def _gather_body(cum_ref, nr_ref, h_ref, pos_ref, xe_hbm, stg, sem, *, cap):
    i = pl.program_id(0)
    nblk = pl.num_programs(0)
    win_rows = COMB_WIN * ROW_TILE

    def copies(blk, r, buf):
        return [pltpu.make_async_copy(
            stg.at[buf, pl.ds(e * win_rows, win_rows), :],
            xe_hbm.at[e, pl.ds(pl.multiple_of((cum_ref[e, blk] + r * COMB_WIN) * ROW_TILE, ROW_TILE), win_rows), :],
            sem.at[buf]) for e in range(N_EXPERTS)]

    w_iota = lax.broadcasted_iota(I32, (COMB_WIN, TOK_BLK), 0)

    def stage(r, buf):
        sel = []
        for e in range(N_EXPERTS):
            nominal = cum_ref[e, i] + r * COMB_WIN
            p = pos_ref[e:e + 1, :]
            sel.append((w_iota == jnp.where(p >= nominal, p - nominal, -1)).astype(BF16))
        res = _dot(jnp.concatenate(sel, axis=0), h_ref[...])
        for j in range(ROW_TILE):
            stg[buf, pl.ds(j, N_EXPERTS * COMB_WIN, stride=ROW_TILE), :] = res[:, j * LANES:(j + 1) * LANES]

    @pl.when(i == 0)
    def _():
        stg[2, pl.ds(0, win_rows), :] = jnp.zeros((win_rows, LANES), F32)
        pads = [pltpu.make_async_copy(stg.at[2, pl.ds(0, win_rows), :],
                                      xe_hbm.at[e, pl.ds(cap * ROW_TILE, win_rows), :], sem.at[2])
                for e in range(N_EXPERTS)]
        for c in pads:
            c.start()
        for c in pads:
            c.wait()

    slot = i % 2
    nr = nr_ref[i]
    stage(0, slot)

    @pl.when((i > 0) & (nr_ref[jnp.maximum(i - 1, 0)] == 1))
    def _():
        for c in copies(i - 1, 0, 1 - slot):
            c.wait()

    for c in copies(i, 0, slot):
        c.start()

    @pl.when(nr > 1)
    def _():
        for c in copies(i, 0, slot):
            c.wait()

        def extra(r, carry):
            stage(r, 2)
            for e, c in enumerate(copies(i, r, 2)):
                @pl.when(cum_ref[e, i + 1] > cum_ref[e, i] + r * COMB_WIN)
                def _():
                    c.start()
                    c.wait()
            return carry

        lax.fori_loop(1, nr, extra, 0)

    @pl.when((i == nblk - 1) & (nr == 1))
    def _():
        for c in copies(i, 0, slot):
            c.wait()


def _gather(cum, nrounds, hn, pos, cap):
    n = hn.shape[0]
    return pl.pallas_call(
        functools.partial(_gather_body, cap=cap),
        grid_spec=pltpu.PrefetchScalarGridSpec(
            num_scalar_prefetch=2,
            grid=(n // TOK_BLK,),
            in_specs=[pl.BlockSpec((TOK_BLK, D_MODEL), lambda i, cum, nr: (i, 0)),
                      pl.BlockSpec((N_EXPERTS, TOK_BLK), lambda i, cum, nr: (0, i))],
            out_specs=pl.BlockSpec(memory_space=pl.ANY),
            scratch_shapes=[pltpu.VMEM((3, N_EXPERTS * COMB_WIN * ROW_TILE, LANES), F32),
                            pltpu.SemaphoreType.DMA((3,))],
        ),
        out_shape=jax.ShapeDtypeStruct((N_EXPERTS, (cap + COMB_WIN) * ROW_TILE, LANES), F32),
        compiler_params=_cparams(("arbitrary",), 40),
        name="expert_gather",
    )(cum, nrounds, hn, pos)


def _ffn_body(x_ref, wg_ref, wu_ref, wd_ref, o_ref, acc_ref, xb_ref):
    f = pl.program_id(2)

    @pl.when(f == 0)
    def _():
        tm = xb_ref.shape[0]
        xb_ref[...] = jnp.concatenate([x_ref[pl.ds(j, tm, stride=ROW_TILE), :] for j in range(ROW_TILE)],
                                      axis=1).astype(BF16)

    x = xb_ref[...]
    gate = _dot(x, wg_ref[...])
    up = _dot(x, wu_ref[...])
    part = _dot((gate * _sigmoid(gate) * up).astype(BF16), wd_ref[...])

    last = pl.num_programs(2) - 1

    @pl.when(f == 0)
    def _():
        acc_ref[...] = part

    @pl.when((f != 0) & (f != last))
    def _():
        acc_ref[...] += part

    @pl.when(f == last)
    def _():
        total = acc_ref[...] + part
        tm = total.shape[0]
        for j in range(ROW_TILE):
            o_ref[pl.ds(j, tm, stride=ROW_TILE), :] = total[:, j * LANES:(j + 1) * LANES]


def _ffn(xe, wg, wu, wd, cap, tm=1024, tf=D_FF_PAD // 2):
    e = xe.shape[0]
    tm = min(tm, cap)
    assert D_FF_PAD // tf >= 2
    return pl.pallas_call(
        _ffn_body,
        grid=(e, cap // tm, D_FF_PAD // tf),
        in_specs=[pl.BlockSpec((None, tm * ROW_TILE, LANES), lambda ei, i, f: (ei, i, 0)),
                  pl.BlockSpec((None, D_MODEL, tf), lambda ei, i, f: (ei, 0, f)),
                  pl.BlockSpec((None, D_MODEL, tf), lambda ei, i, f: (ei, 0, f)),
                  pl.BlockSpec((None, tf, D_MODEL), lambda ei, i, f: (ei, f, 0))],
        out_specs=pl.BlockSpec((None, tm * ROW_TILE, LANES), lambda ei, i, f: (ei, i, 0)),
        out_shape=jax.ShapeDtypeStruct((e, cap * ROW_TILE, LANES), F32),
        scratch_shapes=[pltpu.VMEM((tm, D_MODEL), F32), pltpu.VMEM((tm, D_MODEL), BF16)],
        compiler_params=_cparams(("parallel", "parallel", "arbitrary"), 56),
        name="expert_ffn",
    )(xe, wg, wu, wd)


def _combine_body(cum_ref, nr_ref, x_ref, aff_ref, pos_ref, y_hbm, o_ref, ybuf, sem, *, cap):
    i = pl.program_id(0)
    nblk = pl.num_programs(0)
    win_rows = COMB_WIN * ROW_TILE

    def win_start(blk, e, r):
        return jnp.minimum(cum_ref[e, blk] + r * COMB_WIN, cap - COMB_WIN)

    def copies(blk, r, buf):
        return [pltpu.make_async_copy(
            y_hbm.at[e, pl.ds(pl.multiple_of(win_start(blk, e, r) * ROW_TILE, ROW_TILE), win_rows), :],
            ybuf.at[buf, pl.ds(e * win_rows, win_rows), :],
            sem.at[buf]) for e in range(N_EXPERTS)]

    lane = lax.broadcasted_iota(I32, (TOK_BLK, LANES), 1)
    hi_half = lane >= COMB_WIN
    lane_w = jnp.where(hi_half, lane - COMB_WIN, lane)

    def contribution(r, buf):
        y = jnp.concatenate([ybuf[buf, pl.ds(j, N_EXPERTS * COMB_WIN, stride=ROW_TILE), :]
                             for j in range(ROW_TILE)], axis=1).astype(BF16)
        pos = pos_ref[...]
        aff = aff_ref[...]

        def col(e):
            nominal = cum_ref[e, i] + r * COMB_WIN
            p = pos[:, e:e + 1]
            return jnp.where(p >= nominal, p - win_start(i, e, r), -1)

        parts = []
        for e in range(0, N_EXPERTS, 2):
            d = jnp.where(hi_half, col(e + 1), col(e))
            g = jnp.where(hi_half, aff[:, e + 1:e + 2], aff[:, e:e + 1])
            parts.append(jnp.where(d == lane_w, g, 0.0).astype(BF16))
        return _dot(jnp.concatenate(parts, axis=1), y)

    slot = i % 2

    @pl.when(i == 0)
    def _():
        for c in copies(0, 0, 0):
            c.start()

    @pl.when(i + 1 < nblk)
    def _():
        for c in copies(i + 1, 0, 1 - slot):
            c.start()

    for c in copies(i, 0, slot):
        c.wait()
    o_ref[...] = x_ref[...] + contribution(0, slot)

    def extra(r, carry):
        for c in copies(i, r, 2):
            c.start()
        for c in copies(i, r, 2):
            c.wait()
        o_ref[...] += contribution(r, 2)
        return carry

    lax.fori_loop(1, nr_ref[i], extra, 0)


def _combine(cum, nrounds, x2, aff, pos_t, ye, cap):
    n = x2.shape[0]
    assert 2 * COMB_WIN == LANES and cap >= COMB_WIN
    tok = lambda w: pl.BlockSpec((TOK_BLK, w), lambda i, cum, nr: (i, 0))
    return pl.pallas_call(
        functools.partial(_combine_body, cap=cap),
        grid_spec=pltpu.PrefetchScalarGridSpec(
            num_scalar_prefetch=2,
            grid=(n // TOK_BLK,),
            in_specs=[tok(D_MODEL), tok(N_EXPERTS), tok(N_EXPERTS), pl.BlockSpec(memory_space=pl.ANY)],
            out_specs=tok(D_MODEL),
            scratch_shapes=[pltpu.VMEM((3, N_EXPERTS * COMB_WIN * ROW_TILE, LANES), F32),
                            pltpu.SemaphoreType.DMA((3,))],
        ),
        out_shape=jax.ShapeDtypeStruct((n, D_MODEL), F32),
        compiler_params=_cparams(("arbitrary",), 40),
        name="expert_combine",
    )(cum, nrounds, x2, aff, pos_t, ye)


def _expert_choice(x2, g, w_router, wg, wu, wd):
    n = x2.shape[0]
    cap = EC_CAPACITY * n // N_EXPERTS
    hn, aff = _router(x2, g, w_router)
    aff_t = aff.T
    thr, need = _thresh(aff_t, cap)
    pos = _slots(aff_t, thr, need)
    cnt = jnp.sum((pos >= 0).reshape(N_EXPERTS, n // TOK_BLK, TOK_BLK), axis=-1, dtype=I32)
    cum = jnp.concatenate([jnp.zeros((N_EXPERTS, 1), I32), jnp.cumsum(cnt, axis=1, dtype=I32)], axis=1)
    nrounds = jnp.maximum(1, (jnp.max(cnt, axis=0) + COMB_WIN - 1) // COMB_WIN).astype(I32)
    xe = _gather(cum, nrounds, hn, pos, cap)
    ye = _ffn(xe, wg, wu, wd, cap)
    return _combine(cum, nrounds, x2, aff, pos.T, ye, cap)


def _prep_layer(l, p, lb):
    row = lambda a: a.astype(F32).reshape(1, -1)
    qkg = jnp.concatenate([jnp.tile(p["na_q_norm"][l] * (NA_HD ** -0.5), NA_HEADS),
                           jnp.tile(p["na_k_norm"][l], NA_HEADS)]).reshape(1, 2 * NA_WIDTH)
    idx = jnp.arange(MXU_DIM) // NA_HD
    pad = D_FF_PAD - D_FF
    wgu = p["w_gate_up"][l]
    return dict(
        norm_mix=row(p["norm_mix"][l]), w_in=p["w_in"][l].astype(BF16),
        grp=(idx[:, None] == idx[None, :]).astype(BF16), qkg=qkg.astype(F32),
        bias_tbl=_na_bias_table(p["na_rel_bias"][l]),
        lb4=lb[:, l].reshape(2, HG_HEADS, 1, HG_DK),
        conv_w=p["conv_w"][l].astype(F32), conv_b=row(p["conv_b"][l]),
        ln_g=row(p["conv_ln_g"][l]), ln_b=row(p["conv_ln_b"][l]),
        og=jnp.tile(p["hg_out_norm"][l], HG_HEADS).reshape(1, -1).astype(F32),
        wna=p["w_na_br"][l].astype(BF16), whg=p["w_hg_br"][l].astype(BF16), wcv=p["w_cv_br"][l].astype(BF16),
        wout=p["w_out"][l].astype(BF16),
        norm_mem=row(p["norm_mem"][l]), mem_kv_norm=row(p["mem_kv_norm"][l]),
        wq=p["wq_mem"][l].astype(BF16), wkv=p["wkv_mem"][l].astype(BF16),
        qn=row(p["mem_q_norm"][l]), kn=row(p["mem_k_norm"][l]), wo=p["wo_mem"][l].astype(BF16),
        norm_ffn=row(p["norm_ffn"][l]), w_router=p["w_router"][l].astype(F32),
        wg=jnp.pad(wgu[:, :, :D_FF], ((0, 0), (0, 0), (0, pad))).astype(BF16),
        wu=jnp.pad(wgu[:, :, D_FF:], ((0, 0), (0, 0), (0, pad))).astype(BF16),
        wd=jnp.pad(p["w_down"][l], ((0, 0), (0, pad), (0, 0))).astype(BF16),
    )


def _layer(x3, mem3, w):
    b, l, _ = x3.shape
    n = b * l
    x2 = x3.reshape(n, D_MODEL)
    proj2 = _inproj(x2, w["norm_mix"], w["w_in"], w["grp"], w["qkg"])
    proj3 = proj2.reshape(b, l, D_IN)
    na = _na(proj3, w["bias_tbl"])
    o_f, o_b = _hgrn(proj3, w["lb4"])
    cv = _conv(proj3, w["conv_w"], w["conv_b"], w["ln_g"], w["ln_b"])
    x2 = _merge(na.reshape(n, -1), o_f.reshape(n, -1), o_b.reshape(n, -1), proj2, cv.reshape(n, -1), x2,
                w["og"], w["wna"], w["whg"], w["wcv"], w["wout"])
    kv = _memkv(mem3.reshape(-1, D_MODEL), w["mem_kv_norm"], w["wkv"], w["kn"])
    x3 = _memattn(x2.reshape(b, l, D_MODEL), w["norm_mem"], w["wq"], w["qn"],
                  kv.reshape(b, -1, 2 * MEM_WIDTH), w["wo"])
    x2 = _expert_choice(x3.reshape(n, D_MODEL), w["norm_ffn"], w["w_router"], w["wg"], w["wu"], w["wd"])
    return x2.reshape(b, l, D_MODEL)


def kernel(x_prompt, x_sample, mem_prompt, mem_sample, norm_mix, w_in, na_q_norm, na_k_norm, na_rel_bias, w_na_br, hg_lb, hg_out_norm, w_hg_br, conv_w, conv_b, conv_ln_g, conv_ln_b, w_cv_br, w_out, norm_mem, mem_kv_norm, wq_mem, wkv_mem, mem_q_norm, mem_k_norm, wo_mem, norm_ffn, w_router, w_gate_up, w_down):
    p = dict(norm_mix=norm_mix, w_in=w_in, na_q_norm=na_q_norm, na_k_norm=na_k_norm, na_rel_bias=na_rel_bias,
             w_na_br=w_na_br, hg_out_norm=hg_out_norm, w_hg_br=w_hg_br, conv_w=conv_w, conv_b=conv_b,
             conv_ln_g=conv_ln_g, conv_ln_b=conv_ln_b, w_cv_br=w_cv_br, w_out=w_out, norm_mem=norm_mem,
             mem_kv_norm=mem_kv_norm, wq_mem=wq_mem, wkv_mem=wkv_mem, mem_q_norm=mem_q_norm,
             mem_k_norm=mem_k_norm, wo_mem=wo_mem, norm_ffn=norm_ffn, w_router=w_router, w_gate_up=w_gate_up,
             w_down=w_down)
    depth = w_in.shape[0]
    sm = jax.nn.softmax(hg_lb.astype(F32), axis=1)
    lb = jnp.cumsum(sm, axis=1) - sm[:, :1]
    xs = [x_prompt, x_sample]
    mems = [mem_prompt, mem_sample]
    for l in range(depth):
        w = _prep_layer(l, p, lb)
        xs = [_layer(x, m, w) for x, m in zip(xs, mems)]
    return tuple(xs)
```

```python
import functools

import jax
import jax.numpy as jnp
import numpy as np
from jax import lax
from jax.experimental import pallas as pl
from jax.experimental.pallas import tpu as pltpu

F32 = jnp.float32
BF16 = jnp.bfloat16
I32 = jnp.int32

D_MODEL = 1024
GRID_W = 64
NA_HEADS = 8
NA_HD = 64
NA_WIDTH = NA_HEADS * NA_HD
WIN_R = 8
WIN_C = 16
HG_HEADS = 4
HG_DK = 128
HG_DV = 128
HG_KW = HG_HEADS * HG_DK
F_MIN = 1e-20
D_CONV = 512
CONV_K = 31
MEM_HEADS = 4
MEM_HD = 128
MEM_WIDTH = MEM_HEADS * MEM_HD
N_EXPERTS = 16
EC_CAPACITY = 2
D_FF = 2752
EPS = 1e-6
D_IN = 8192

COL_NA_Q, COL_NA_K, COL_NA_V = 0, 1, 2
COL_HG_Q, COL_HG_FF, COL_HG_FB, COL_HG_V, COL_HG_G = 3, 4, 5, 6, 7
COL_CV_A, COL_CV_B = 8, 9
COL_G_NA, COL_G_HG, COL_G_CV = 5, 6, 7

LANES = 128
SUBLANES = 8
MXU_DIM = 256
D_FF_PAD = 2816
NEG_BIG = -1e30
LOG2E = 1.4426950408889634

HG_CHUNK = 256
TOK_BLK = 256
ROW_TILE = D_MODEL // LANES
COMB_WIN = 64


def _cparams(sem, vmem_mb):
    return pltpu.CompilerParams(dimension_semantics=sem, vmem_limit_bytes=vmem_mb << 20)


def _sigmoid_pair(x):
    t = 0.5 * jnp.tanh(0.5 * x)
    return 0.5 + t, 0.5 - t


def _sigmoid(x):
    return 0.5 + 0.5 * jnp.tanh(0.5 * x)


def _neg_abs(x):
    return lax.bitcast_convert_type(lax.bitcast_convert_type(x, I32) | jnp.int32(-2 ** 31), F32)


def _split2(a):
    hi = a.astype(BF16)
    lo = (a - hi.astype(F32)).astype(BF16)
    return hi, lo


def _dot(a, b):
    return jnp.dot(a, b, preferred_element_type=F32)


def _dot_nt(a, b):
    return lax.dot_general(a, b, (((1,), (1,)), ((), ())), preferred_element_type=F32)


def _dot_tn(a, b):
    return lax.dot_general(a, b, (((0,), (0,)), ((), ())), preferred_element_type=F32)


def _rms_rows(x, g):
    return x * lax.rsqrt(jnp.mean(x * x, axis=-1, keepdims=True) + EPS) * g


def _inproj_body(x_ref, g_ref, w_ref, grp_ref, qkg_ref, o_ref, h_ref):
    j = pl.program_id(1)

    @pl.when(j == 0)
    def _():
        h_ref[...] = _rms_rows(x_ref[...], g_ref[...]).astype(BF16)

    acc = _dot(h_ref[...], w_ref[...])

    @pl.when(j == 0)
    def _():
        nqk = qkg_ref.shape[1]
        qk = acc[:, :nqk]
        sq = (qk * qk).astype(BF16)
        w = grp_ref.shape[0]
        ss = jnp.concatenate([_dot(sq[:, c * w:(c + 1) * w], grp_ref[...]) for c in range(nqk // w)], axis=1)
        o_ref[:, :nqk] = (qk * lax.rsqrt(ss * (1.0 / NA_HD) + EPS) * qkg_ref[...]).astype(BF16)
        o_ref[:, nqk:] = acc[:, nqk:].astype(BF16)

    @pl.when(j != 0)
    def _():
        o_ref[...] = acc.astype(BF16)


def _inproj(x2, g, w_bf, grp, qkg, tm=1024, tn=2048):
    n = x2.shape[0]
    assert tn > qkg.shape[1]
    return pl.pallas_call(
        _inproj_body,
        grid=(n // tm, D_IN // tn),
        in_specs=[
            pl.BlockSpec((tm, D_MODEL), lambda i, j: (i, 0)),
            pl.BlockSpec((1, D_MODEL), lambda i, j: (0, 0)),
            pl.BlockSpec((D_MODEL, tn), lambda i, j: (0, j)),
            pl.BlockSpec((MXU_DIM, MXU_DIM), lambda i, j: (0, 0)),
            pl.BlockSpec((1, 2 * NA_WIDTH), lambda i, j: (0, 0)),
        ],
        out_specs=pl.BlockSpec((tm, tn), lambda i, j: (i, j)),
        out_shape=jax.ShapeDtypeStruct((n, D_IN), BF16),
        scratch_shapes=[pltpu.VMEM((tm, D_MODEL), BF16)],
        compiler_params=_cparams(("parallel", "arbitrary"), 48),
        name="in_proj",
    )(x2, g, w_bf, grp, qkg)


NA_QROWS = 4
NA_KROWS = NA_QROWS + WIN_R


def _na_body(q_ref, k_ref, v_ref, bias_ref, o_ref):
    q = q_ref[...]
    k = k_ref[0]
    v = v_ref[0]
    outs = []
    for h in range(NA_HEADS):
        sl = slice(h * NA_HD, (h + 1) * NA_HD)
        s = _dot_nt(q[:, sl], k[:, sl]) + bias_ref[h]
        m = jnp.max(s, axis=-1, keepdims=True)
        p = jnp.exp(s - m)
        l = jnp.sum(p, axis=-1, keepdims=True)
        outs.append(_dot(p.astype(BF16), v[:, sl]) / l)
    o_ref[...] = jnp.concatenate(outs, axis=1).astype(BF16)


def _na(proj3, bias_tbl):
    b, l, _ = proj3.shape
    rows = l // GRID_W
    assert rows % NA_QROWS == 0 and rows >= 2 * WIN_R
    nblk = rows // NA_QROWS

    def key_row0(g):
        return jnp.clip(NA_QROWS * g - WIN_R // 2, 0, rows - NA_KROWS)

    def variant(g):
        return jnp.where(g == 0, 0, jnp.where(g == nblk - 1, 2, 1))

    qblk = (None, NA_QROWS * GRID_W, NA_WIDTH)
    kblk = (pl.Element(1), pl.Element(NA_KROWS * GRID_W), pl.Element(NA_WIDTH))
    return pl.pallas_call(
        _na_body,
        grid=(b, nblk),
        in_specs=[pl.BlockSpec(qblk, lambda bi, g: (bi, g, COL_NA_Q)),
                  pl.BlockSpec(kblk, lambda bi, g: (bi, key_row0(g) * GRID_W, COL_NA_K * NA_WIDTH)),
                  pl.BlockSpec(kblk, lambda bi, g: (bi, key_row0(g) * GRID_W, COL_NA_V * NA_WIDTH)),
                  pl.BlockSpec((None, NA_HEADS, NA_QROWS * GRID_W, NA_KROWS * GRID_W),
                               lambda bi, g: (variant(g), 0, 0, 0))],
        out_specs=pl.BlockSpec(qblk, lambda bi, g: (bi, g, 0)),
        out_shape=jax.ShapeDtypeStruct((b, l, NA_WIDTH), BF16),
        compiler_params=_cparams(("parallel", "arbitrary"), 48),
        name="na_attention",
    )(proj3, proj3, proj3, bias_tbl)


def _na_bias_table(rel_bias):
    c = np.arange(GRID_W)
    c0 = np.clip(c - WIN_C // 2, 0, GRID_W - WIN_C)
    kc = np.arange(GRID_W)
    col_ok = (kc[None, :] >= c0[:, None]) & (kc[None, :] < c0[:, None] + WIN_C)
    dcol = kc[None, :] - c[:, None] + WIN_C - 1
    pick = (dcol[:, :, None] == np.arange(2 * WIN_C - 1)[None, None, :]) & col_ok[:, :, None]
    toe = jnp.einsum('hrd,ckd->hrck', rel_bias.astype(F32), jnp.asarray(pick, F32),
                     precision=lax.Precision.HIGHEST)
    toe = jnp.where(jnp.asarray(col_ok)[None, None], toe, NEG_BIG)
    pad = NA_KROWS
    toe = jnp.pad(toe, ((0, 0), (pad, pad), (0, 0), (0, 0)), constant_values=NEG_BIG)
    i = np.arange(NA_KROWS)
    variants = []
    for var in range(3):
        per_row = []
        for rho in range(NA_QROWS):
            off = {0: rho, 1: rho + WIN_R // 2, 2: rho + WIN_R}[var]
            lo = {0: 0, 1: rho, 2: NA_QROWS}[var]
            row_ok = (i >= lo) & (i < lo + WIN_R)
            dr0 = -off + WIN_R - 1 + pad
            t = toe[:, dr0:dr0 + NA_KROWS]
            t = jnp.where(jnp.asarray(row_ok)[None, :, None, None], t, NEG_BIG)
            per_row.append(t.transpose(0, 2, 1, 3).reshape(NA_HEADS, GRID_W, NA_KROWS * GRID_W))
        variants.append(jnp.concatenate(per_row, axis=1))
    return jnp.stack(variants)


def _hgrn_chunk(hq, hf, v, lb, st_ref, rev):
    c = hq.shape[0]
    hq = hq.astype(F32)
    q = hq * _sigmoid(hq) * (HG_DK ** -0.5)
    s_pos, s_neg = _sigmoid_pair(hf.astype(F32))
    f = lb + (1.0 - lb) * s_pos
    logf = jnp.log(jnp.maximum(f, F_MIN))
    k = (1.0 - lb) * s_neg

    hb = c // 2
    t_col = lax.broadcasted_iota(I32, (c, 1), 0)
    t_mat = lax.broadcasted_iota(I32, (c, c), 0)
    s_mat = lax.broadcasted_iota(I32, (c, c), 1)
    tri = (s_mat >= t_mat) if rev else (s_mat <= t_mat)
    hi, lo = _split2(logf)
    cs = _dot(tri.astype(BF16), jnp.concatenate([hi, lo], axis=1))
    a = (cs[:, :HG_DK] + cs[:, HG_DK:]) * LOG2E

    st = st_ref[...]
    o_state = _dot_nt((q * jnp.exp2(a)).astype(BF16), st.astype(BF16))

    tl = lax.broadcasted_iota(I32, (hb, hb), 0)
    sl = lax.broadcasted_iota(I32, (hb, hb), 1)
    x = tl ^ sl
    keep = (tl <= sl) if rev else (tl >= sl)
    dg = jnp.sum(q * k, axis=1, keepdims=True)
    att = [jnp.where(x == 0, dg[b * hb:(b + 1) * hb], 0.0) for b in range(2)]
    a8 = a.reshape(c // 8, 8, HG_DK)
    sub = lax.broadcasted_iota(I32, (1, 8, 1), 1)
    m = 1
    while m < hb:
        if m >= 4:
            ar = a.reshape(c // (2 * m), 2 * m, HG_DK)
            row = m if rev else m - 1
            ab = jnp.broadcast_to(ar[:, row:row + 1, :], ar.shape).reshape(c, HG_DK)
        else:
            off = m if rev else m - 1
            rows = [a8[:, p + off:p + off + 1, :] for p in range(0, 8, 2 * m)]
            ab = rows[-1]
            for idx in range(len(rows) - 2, -1, -1):
                ab = jnp.where(sub < (idx + 1) * 2 * m, rows[idx], ab)
            ab = jnp.broadcast_to(ab, a8.shape).reshape(c, HG_DK)
        e = jnp.exp2(_neg_abs(a - ab))
        upper = (t_col & m) != 0
        q_half = jnp.logical_not(upper) if rev else upper
        z = (jnp.where(q_half, q, k) * e).astype(BF16)
        for b in range(2):
            zb = z[b * hb:(b + 1) * hb]
            att[b] = jnp.where(x >= m, _dot_nt(zb, zb), att[b])
        m *= 2
    att = [jnp.where(keep, t, 0.0).astype(BF16) for t in att]

    qs, ks = (slice(0, hb), slice(hb, c)) if rev else (slice(hb, c), slice(0, hb))
    ab = a[hb:hb + 1] if rev else a[hb - 1:hb]
    qe = (q[qs] * jnp.exp2(_neg_abs(a[qs] - ab))).astype(BF16)
    ke = (k[ks] * jnp.exp2(_neg_abs(a[ks] - ab))).astype(BF16)
    cross = _dot(_dot_nt(qe, ke).astype(BF16), v[ks])
    o0 = _dot(att[0], v[:hb])
    o1 = _dot(att[1], v[hb:])
    o = o_state + (jnp.concatenate([o0 + cross, o1], axis=0) if rev else jnp.concatenate([o0, o1 + cross], axis=0))

    a_end = a[0:1, :] if rev else a[c - 1:c, :]
    k_end = (k * jnp.exp2(a_end - a)).astype(BF16)
    st_ref[...] = st * jnp.exp2(a_end) + _dot_tn(v, k_end)
    return o


def _hgrn_body(qf_ref, ff_ref, vf_ref, qb_ref, fb_ref, vb_ref, lb_ref, of_ref, ob_ref, st_ref):
    @pl.when(pl.program_id(1) == 0)
    def _():
        st_ref[...] = jnp.zeros_like(st_ref)

    for h in range(HG_HEADS):
        sl = slice(h * HG_DK, (h + 1) * HG_DK)
        of_ref[:, sl] = _hgrn_chunk(qf_ref[:, sl], ff_ref[:, sl], vf_ref[:, sl], lb_ref[0, h], st_ref.at[0, h], False)
        ob_ref[:, sl] = _hgrn_chunk(qb_ref[:, sl], fb_ref[:, sl], vb_ref[:, sl], lb_ref[1, h], st_ref.at[1, h], True)


def _hgrn(proj3, lb4, chunk=HG_CHUNK):
    b, l, _ = proj3.shape
    n = l // chunk
    width = HG_HEADS * HG_DK

    def spec(col, back):
        if back:
            return pl.BlockSpec((None, chunk, width), lambda bi, i: (bi, n - 1 - i, col))
        return pl.BlockSpec((None, chunk, width), lambda bi, i: (bi, i, col))

    return pl.pallas_call(
        _hgrn_body,
        grid=(b, n),
        in_specs=[spec(COL_HG_Q, False), spec(COL_HG_FF, False), spec(COL_HG_V, False),
                  spec(COL_HG_Q, True), spec(COL_HG_FB, True), spec(COL_HG_V, True),
                  pl.BlockSpec((2, HG_HEADS, 1, HG_DK), lambda bi, i: (0, 0, 0, 0))],
        out_specs=[spec(0, False), spec(0, True)],
        out_shape=[jax.ShapeDtypeStruct((b, l, HG_HEADS * HG_DV), F32)] * 2,
        scratch_shapes=[pltpu.VMEM((2, HG_HEADS, HG_DV, HG_DK), F32)],
        compiler_params=_cparams(("parallel", "arbitrary"), 32),
        name="hgrn2",
    )(*([proj3] * 6), lb4)


CONV_HALO = 16


def _conv_body(a_ref, b_ref, ap_ref, bp_ref, an_ref, bn_ref, w_ref, cb_ref, g_ref, be_ref, o_ref, u_ref, sh_ref):
    i = pl.program_id(1)
    n = pl.num_programs(1)
    t = a_ref.shape[0]

    def glu(a, b):
        return a[...].astype(F32) * _sigmoid(b[...].astype(F32))

    u_ref[pl.ds(0, CONV_HALO), :] = jnp.where(i > 0, glu(ap_ref, bp_ref), 0.0)
    u_ref[pl.ds(CONV_HALO, t), :] = glu(a_ref, b_ref)
    u_ref[pl.ds(CONV_HALO + t, CONV_HALO), :] = jnp.where(i < n - 1, glu(an_ref, bn_ref), 0.0)
    span = t + 2 * CONV_HALO - SUBLANES
    for r in range(1, SUBLANES):
        sh_ref[r - 1] = u_ref[pl.ds(r, span), :]
    base = CONV_HALO - CONV_K // 2
    acc = jnp.zeros((t, D_CONV), F32)
    for kk in range(CONV_K):
        off = base + kk
        r = off % SUBLANES
        win = u_ref[pl.ds(off, t), :] if r == 0 else sh_ref[r - 1, pl.ds(off - r, t), :]
        acc = acc + w_ref[kk:kk + 1, :] * win
    y = acc + cb_ref[...]
    mu = jnp.mean(y, axis=-1, keepdims=True)
    yc = y - mu
    var = jnp.mean(yc * yc, axis=-1, keepdims=True)
    yn = yc * lax.rsqrt(var + EPS) * g_ref[...] + be_ref[...]
    o_ref[...] = (yn * _sigmoid(yn)).astype(BF16)


def _conv(proj3, conv_w, conv_b, ln_g, ln_b, t=512):
    b, l, _ = proj3.shape
    nt = l // t
    hb = t // CONV_HALO
    nh = l // CONV_HALO
    main = lambda col: pl.BlockSpec((None, t, D_CONV), lambda bi, i: (bi, i, col))
    prev = lambda col: pl.BlockSpec((None, CONV_HALO, D_CONV), lambda bi, i: (bi, jnp.maximum(i * hb - 1, 0), col))
    nxt = lambda col: pl.BlockSpec((None, CONV_HALO, D_CONV), lambda bi, i: (bi, jnp.minimum((i + 1) * hb, nh - 1), col))
    vec = pl.BlockSpec((1, D_CONV), lambda bi, i: (0, 0))
    return pl.pallas_call(
        _conv_body,
        grid=(b, nt),
        in_specs=[main(COL_CV_A), main(COL_CV_B), prev(COL_CV_A), prev(COL_CV_B), nxt(COL_CV_A), nxt(COL_CV_B),
                  pl.BlockSpec((CONV_K, D_CONV), lambda bi, i: (0, 0)), vec, vec, vec],
        out_specs=pl.BlockSpec((None, t, D_CONV), lambda bi, i: (bi, i, 0)),
        out_shape=jax.ShapeDtypeStruct((b, l, D_CONV), BF16),
        scratch_shapes=[pltpu.VMEM((t + 2 * CONV_HALO, D_CONV), F32),
                        pltpu.VMEM((SUBLANES - 1, t + 2 * CONV_HALO - SUBLANES, D_CONV), F32)],
        compiler_params=_cparams(("parallel", "arbitrary"), 32),
        name="conv_module",
    )(*([proj3] * 6), conv_w, conv_b, ln_g, ln_b)


def _merge_body(na_ref, of_ref, ob_ref, hg_ref, cv_ref, gna_ref, ghg_ref, gcv_ref, x_ref,
                og_ref, wna_ref, whg_ref, wcv_ref, wout_ref, o_ref):
    o = of_ref[...] + ob_ref[...]
    parts = []
    for h in range(HG_HEADS):
        oh = o[:, h * HG_DV:(h + 1) * HG_DV]
        parts.append(oh * lax.rsqrt(jnp.mean(oh * oh, axis=-1, keepdims=True) + EPS))
    gg = hg_ref[...].astype(F32)
    hg = jnp.concatenate(parts, axis=1) * og_ref[...] * (gg * _sigmoid(gg))
    merged = (_sigmoid(gna_ref[...].astype(F32)) * _dot(na_ref[...], wna_ref[...])
              + _sigmoid(ghg_ref[...].astype(F32)) * _dot(hg.astype(BF16), whg_ref[...])
              + _sigmoid(gcv_ref[...].astype(F32)) * _dot(cv_ref[...], wcv_ref[...]))
    o_ref[...] = x_ref[...] + _dot(merged.astype(BF16), wout_ref[...])


def _merge(na2, of2, ob2, proj2, cv2, x2, og, wna, whg, wcv, wout, t=512):
    n = x2.shape[0]
    half = lambda col=0: pl.BlockSpec((t, 512), lambda i: (i, col))
    full = lambda col=0: pl.BlockSpec((t, D_MODEL), lambda i: (i, col))
    wspec = lambda r: pl.BlockSpec((r, D_MODEL), lambda i: (0, 0))
    return pl.pallas_call(
        _merge_body,
        grid=(n // t,),
        in_specs=[half(), half(), half(), half(COL_HG_G), half(), full(COL_G_NA), full(COL_G_HG), full(COL_G_CV),
                  full(), pl.BlockSpec((1, 512), lambda i: (0, 0)), wspec(512), wspec(512), wspec(512),
                  wspec(D_MODEL)],
        out_specs=full(),
        out_shape=jax.ShapeDtypeStruct((n, D_MODEL), F32),
        compiler_params=_cparams(("parallel",), 48),
        name="merge_out_proj",
    )(na2, of2, ob2, proj2, cv2, proj2, proj2, proj2, x2, og, wna, whg, wcv, wout)


def _memkv_body(m_ref, g_ref, w_ref, kn_ref, o_ref):
    kv = _dot(_rms_rows(m_ref[...], g_ref[...]).astype(BF16), w_ref[...])
    parts = []
    for h in range(MEM_HEADS):
        kh = kv[:, h * MEM_HD:(h + 1) * MEM_HD]
        parts.append(_rms_rows(kh, kn_ref[...]))
    parts.append(kv[:, MEM_WIDTH:])
    o_ref[...] = jnp.concatenate(parts, axis=1).astype(BF16)


def _memkv(mem2, g, wkv, kn, t=256):
    n = mem2.shape[0]
    return pl.pallas_call(
        _memkv_body,
        grid=(n // t,),
        in_specs=[pl.BlockSpec((t, D_MODEL), lambda i: (i, 0)), pl.BlockSpec((1, D_MODEL), lambda i: (0, 0)),
                  pl.BlockSpec((D_MODEL, 2 * MEM_WIDTH), lambda i: (0, 0)),
                  pl.BlockSpec((1, MEM_HD), lambda i: (0, 0))],
        out_specs=pl.BlockSpec((t, 2 * MEM_WIDTH), lambda i: (i, 0)),
        out_shape=jax.ShapeDtypeStruct((n, 2 * MEM_WIDTH), BF16),
        compiler_params=_cparams(("parallel",), 32),
        name="mem_kv",
    )(mem2, g, wkv, kn)


def _memattn_body(x_ref, g_ref, wq_ref, qn_ref, kv_ref, wo_ref, gf_ref, wr_ref, o_ref, h_ref, a_ref):
    x = x_ref[...]
    q = _dot(_rms_rows(x, g_ref[...]).astype(BF16), wq_ref[...])
    kv = kv_ref[...]
    outs = []
    for h in range(MEM_HEADS):
        sl = slice(h * MEM_HD, (h + 1) * MEM_HD)
        qh = _rms_rows(q[:, sl], qn_ref[...]) * (MEM_HD ** -0.5)
        s = _dot_nt(qh.astype(BF16), kv[:, sl])
        m = jnp.max(s, axis=-1, keepdims=True)
        p = jnp.exp(s - m)
        l = jnp.sum(p, axis=-1, keepdims=True)
        outs.append(_dot(p.astype(BF16), kv[:, MEM_WIDTH + h * MEM_HD:MEM_WIDTH + (h + 1) * MEM_HD]) / l)
    xo = x + _dot(jnp.concatenate(outs, axis=1).astype(BF16), wo_ref[...])
    o_ref[...] = xo
    hn = _rms_rows(xo, gf_ref[...])
    h_ref[...] = hn.astype(BF16)
    h_hi, h_lo = _split2(hn)
    w_hi, w_lo = _split2(wr_ref[...])
    logits = _dot(h_hi, w_hi) + _dot(h_hi, w_lo) + _dot(h_lo, w_hi)
    e = jnp.exp(logits - jnp.max(logits, axis=-1, keepdims=True))
    a_ref[...] = e / jnp.sum(e, axis=-1, keepdims=True)


def _memattn(x3, g, wq, qn, kv3, wo, g_ffn, w_router, t=512):
    b, l, _ = x3.shape
    m = kv3.shape[1]
    tok = lambda w: pl.BlockSpec((None, t, w), lambda bi, i: (bi, i, 0))
    const = lambda r, c: pl.BlockSpec((r, c), lambda bi, i: (0, 0))
    return pl.pallas_call(
        _memattn_body,
        grid=(b, l // t),
        in_specs=[tok(D_MODEL), const(1, D_MODEL), const(D_MODEL, MEM_WIDTH), const(1, MEM_HD),
                  pl.BlockSpec((None, m, 2 * MEM_WIDTH), lambda bi, i: (bi, 0, 0)),
                  const(MEM_WIDTH, D_MODEL), const(1, D_MODEL), const(D_MODEL, N_EXPERTS)],
        out_specs=[tok(D_MODEL), tok(D_MODEL), tok(N_EXPERTS)],
        out_shape=[jax.ShapeDtypeStruct((b, l, D_MODEL), F32), jax.ShapeDtypeStruct((b, l, D_MODEL), BF16),
                   jax.ShapeDtypeStruct((b, l, N_EXPERTS), F32)],
        compiler_params=_cparams(("parallel", "arbitrary"), 32),
        name="mem_attention_router",
    )(x3, g, wq, qn, kv3, wo, g_ffn, w_router)


def _thresh_body(a_ref, thr_ref, need_ref, *, cap):
    bits = pltpu.bitcast(a_ref[...], I32)

    def step(b, thr):
        cand = thr | (1 << (30 - b))
        cnt = jnp.sum((bits >= cand).astype(I32), axis=1, keepdims=True)
        return jnp.where(cnt >= cap, cand, thr)

    thr = lax.fori_loop(0, 31, step, jnp.zeros((N_EXPERTS, 1), I32))
    thr_ref[...] = thr
    need_ref[...] = cap - jnp.sum((bits > thr).astype(I32), axis=1, keepdims=True)


def _thresh(aff_t, cap):
    n = aff_t.shape[1]
    return pl.pallas_call(
        functools.partial(_thresh_body, cap=cap),
        grid=(1,),
        in_specs=[pl.BlockSpec((N_EXPERTS, n), lambda i: (0, 0))],
        out_specs=[pl.BlockSpec((N_EXPERTS, 1), lambda i: (0, 0))] * 2,
        out_shape=[jax.ShapeDtypeStruct((N_EXPERTS, 1), I32)] * 2,
        compiler_params=_cparams(("arbitrary",), 32),
        name="topk_threshold",
    )(aff_t)


def _slots_body(a_ref, thr_ref, need_ref, pos_ref, eq_off, sel_off):
    @pl.when(pl.program_id(0) == 0)
    def _():
        eq_off[...] = jnp.zeros_like(eq_off)
        sel_off[...] = jnp.zeros_like(sel_off)

    bits = pltpu.bitcast(a_ref[...], I32)
    w = bits.shape[1]
    thr = thr_ref[...]
    before = (lax.broadcasted_iota(I32, (w, w), 0) < lax.broadcasted_iota(I32, (w, w), 1)).astype(BF16)
    eq = bits == thr
    eq_f = eq.astype(F32)
    eq_rank = _dot(eq_f.astype(BF16), before) + eq_off[...]
    sel = (bits > thr) | (eq & (eq_rank < need_ref[...].astype(F32)))
    sel_f = sel.astype(F32)
    slot = _dot(sel_f.astype(BF16), before) + sel_off[...]
    pos_ref[...] = jnp.where(sel, slot, -1.0).astype(I32)
    eq_off[...] += jnp.sum(eq_f, axis=1, keepdims=True)
    sel_off[...] += jnp.sum(sel_f, axis=1, keepdims=True)


def _slots(aff_t, thr, need, w=TOK_BLK):
    n = aff_t.shape[1]
    col = pl.BlockSpec((N_EXPERTS, 1), lambda i: (0, 0))
    return pl.pallas_call(
        _slots_body,
        grid=(n // w,),
        in_specs=[pl.BlockSpec((N_EXPERTS, w), lambda i: (0, i)), col, col],
        out_specs=pl.BlockSpec((N_EXPERTS, w), lambda i: (0, i)),
        out_shape=jax.ShapeDtypeStruct((N_EXPERTS, n), I32),
        scratch_shapes=[pltpu.VMEM((N_EXPERTS, 1), F32)] * 2,
        compiler_params=_cparams(("arbitrary",), 32),
        name="topk_slots",
    )(aff_t, thr, need)


def _gather_body(cum_ref, nr_ref, h_ref, pos_ref, xe_hbm, stg, sem, *, cap):
    i = pl.program_id(0)
    nblk = pl.num_programs(0)
    win_rows = COMB_WIN * ROW_TILE

    def copies(blk, r, buf):
        return [pltpu.make_async_copy(
            stg.at[buf, pl.ds(e * win_rows, win_rows), :],
            xe_hbm.at[e, pl.ds(pl.multiple_of((cum_ref[e, blk] + r * COMB_WIN) * ROW_TILE, ROW_TILE), win_rows), :],
            sem.at[buf]) for e in range(N_EXPERTS)]

    w_iota = lax.broadcasted_iota(I32, (COMB_WIN, TOK_BLK), 0)

    def stage(r, buf):
        sel = []
        for e in range(N_EXPERTS):
            nominal = cum_ref[e, i] + r * COMB_WIN
            p = pos_ref[e:e + 1, :]
            sel.append((w_iota == jnp.where(p >= nominal, p - nominal, -1)).astype(BF16))
        res = _dot(jnp.concatenate(sel, axis=0), h_ref[...])
        for j in range(ROW_TILE):
            stg[buf, pl.ds(j, N_EXPERTS * COMB_WIN, stride=ROW_TILE), :] = res[:, j * LANES:(j + 1) * LANES]

    @pl.when(i == 0)
    def _():
        stg[2, pl.ds(0, win_rows), :] = jnp.zeros((win_rows, LANES), F32)
        pads = [pltpu.make_async_copy(stg.at[2, pl.ds(0, win_rows), :],
                                      xe_hbm.at[e, pl.ds(cap * ROW_TILE, win_rows), :], sem.at[2])
                for e in range(N_EXPERTS)]
        for c in pads:
            c.start()
        for c in pads:
            c.wait()

    slot = i % 2
    nr = nr_ref[i]
    stage(0, slot)

    @pl.when((i > 0) & (nr_ref[jnp.maximum(i - 1, 0)] == 1))
    def _():
        for c in copies(i - 1, 0, 1 - slot):
            c.wait()

    for c in copies(i, 0, slot):
        c.start()

    @pl.when(nr > 1)
    def _():
        for c in copies(i, 0, slot):
            c.wait()

        def extra(r, carry):
            stage(r, 2)
            for e, c in enumerate(copies(i, r, 2)):
                @pl.when(cum_ref[e, i + 1] > cum_ref[e, i] + r * COMB_WIN)
                def _():
                    c.start()
                    c.wait()
            return carry

        lax.fori_loop(1, nr, extra, 0)

    @pl.when((i == nblk - 1) & (nr == 1))
    def _():
        for c in copies(i, 0, slot):
            c.wait()


def _gather(cum, nrounds, hn, pos, cap):
    n = hn.shape[0]
    return pl.pallas_call(
        functools.partial(_gather_body, cap=cap),
        grid_spec=pltpu.PrefetchScalarGridSpec(
            num_scalar_prefetch=2,
            grid=(n // TOK_BLK,),
            in_specs=[pl.BlockSpec((TOK_BLK, D_MODEL), lambda i, cum, nr: (i, 0)),
                      pl.BlockSpec((N_EXPERTS, TOK_BLK), lambda i, cum, nr: (0, i))],
            out_specs=pl.BlockSpec(memory_space=pl.ANY),
            scratch_shapes=[pltpu.VMEM((3, N_EXPERTS * COMB_WIN * ROW_TILE, LANES), F32),
                            pltpu.SemaphoreType.DMA((3,))],
        ),
        out_shape=jax.ShapeDtypeStruct((N_EXPERTS, (cap + COMB_WIN) * ROW_TILE, LANES), F32),
        compiler_params=_cparams(("arbitrary",), 40),
        name="expert_gather",
    )(cum, nrounds, hn, pos)


def _ffn_body(x_ref, wg_ref, wu_ref, wd_ref, o_ref, acc_ref, xb_ref):
    f = pl.program_id(2)

    @pl.when(f == 0)
    def _():
        tm = xb_ref.shape[0]
        xb_ref[...] = jnp.concatenate([x_ref[pl.ds(j, tm, stride=ROW_TILE), :] for j in range(ROW_TILE)],
                                      axis=1).astype(BF16)

    x = xb_ref[...]
    gate = _dot(x, wg_ref[...])
    up = _dot(x, wu_ref[...])
    part = _dot((gate * _sigmoid(gate) * up).astype(BF16), wd_ref[...])

    last = pl.num_programs(2) - 1

    @pl.when(f == 0)
    def _():
        acc_ref[...] = part

    @pl.when((f != 0) & (f != last))
    def _():
        acc_ref[...] += part

    @pl.when(f == last)
    def _():
        total = acc_ref[...] + part
        tm = total.shape[0]
        for j in range(ROW_TILE):
            o_ref[pl.ds(j, tm, stride=ROW_TILE), :] = total[:, j * LANES:(j + 1) * LANES]


def _ffn(xe, wgu, wd, cap, tm=1024, tf=D_FF_PAD // 2):
    e = xe.shape[0]
    tm = min(tm, cap)
    nf = D_FF_PAD // tf
    assert nf >= 2
    return pl.pallas_call(
        _ffn_body,
        grid=(e, cap // tm, nf),
        in_specs=[pl.BlockSpec((None, tm * ROW_TILE, LANES), lambda ei, i, f: (ei, i, 0)),
                  pl.BlockSpec((None, D_MODEL, tf), lambda ei, i, f: (ei, 0, f)),
                  pl.BlockSpec((None, D_MODEL, tf), lambda ei, i, f: (ei, 0, f + nf)),
                  pl.BlockSpec((None, tf, D_MODEL), lambda ei, i, f: (ei, f, 0))],
        out_specs=pl.BlockSpec((None, tm * ROW_TILE, LANES), lambda ei, i, f: (ei, i, 0)),
        out_shape=jax.ShapeDtypeStruct((e, cap * ROW_TILE, LANES), F32),
        scratch_shapes=[pltpu.VMEM((tm, D_MODEL), F32), pltpu.VMEM((tm, D_MODEL), BF16)],
        compiler_params=_cparams(("parallel", "parallel", "arbitrary"), 56),
        name="expert_ffn",
    )(xe, wgu, wgu, wd)


def _combine_body(cum_ref, nr_ref, x_ref, aff_ref, pos_ref, y_hbm, o_ref, ybuf, sem, *, cap):
    i = pl.program_id(0)
    nblk = pl.num_programs(0)
    win_rows = COMB_WIN * ROW_TILE

    def win_start(blk, e, r):
        return jnp.minimum(cum_ref[e, blk] + r * COMB_WIN, cap - COMB_WIN)

    def copies(blk, r, buf):
        return [pltpu.make_async_copy(
            y_hbm.at[e, pl.ds(pl.multiple_of(win_start(blk, e, r) * ROW_TILE, ROW_TILE), win_rows), :],
            ybuf.at[buf, pl.ds(e * win_rows, win_rows), :],
            sem.at[buf]) for e in range(N_EXPERTS)]

    lane = lax.broadcasted_iota(I32, (TOK_BLK, LANES), 1)
    hi_half = lane >= COMB_WIN
    lane_w = jnp.where(hi_half, lane - COMB_WIN, lane)

    def contribution(r, buf):
        y = jnp.concatenate([ybuf[buf, pl.ds(j, N_EXPERTS * COMB_WIN, stride=ROW_TILE), :]
                             for j in range(ROW_TILE)], axis=1).astype(BF16)
        pos = pos_ref[...]
        aff = aff_ref[...]

        def col(e):
            nominal = cum_ref[e, i] + r * COMB_WIN
            p = pos[:, e:e + 1]
            return jnp.where(p >= nominal, p - win_start(i, e, r), -1)

        parts = []
        for e in range(0, N_EXPERTS, 2):
            d = jnp.where(hi_half, col(e + 1), col(e))
            g = jnp.where(hi_half, aff[:, e + 1:e + 2], aff[:, e:e + 1])
            parts.append(jnp.where(d == lane_w, g, 0.0).astype(BF16))
        return _dot(jnp.concatenate(parts, axis=1), y)

    slot = i % 2

    @pl.when(i == 0)
    def _():
        for c in copies(0, 0, 0):
            c.start()

    @pl.when(i + 1 < nblk)
    def _():
        for c in copies(i + 1, 0, 1 - slot):
            c.start()

    for c in copies(i, 0, slot):
        c.wait()
    o_ref[...] = x_ref[...] + contribution(0, slot)

    def extra(r, carry):
        for c in copies(i, r, 2):
            c.start()
        for c in copies(i, r, 2):
            c.wait()
        o_ref[...] += contribution(r, 2)
        return carry

    lax.fori_loop(1, nr_ref[i], extra, 0)


def _combine(cum, nrounds, x2, aff, pos_t, ye, cap):
    n = x2.shape[0]
    assert 2 * COMB_WIN == LANES and cap >= COMB_WIN
    tok = lambda w: pl.BlockSpec((TOK_BLK, w), lambda i, cum, nr: (i, 0))
    return pl.pallas_call(
        functools.partial(_combine_body, cap=cap),
        grid_spec=pltpu.PrefetchScalarGridSpec(
            num_scalar_prefetch=2,
            grid=(n // TOK_BLK,),
            in_specs=[tok(D_MODEL), tok(N_EXPERTS), tok(N_EXPERTS), pl.BlockSpec(memory_space=pl.ANY)],
            out_specs=tok(D_MODEL),
            scratch_shapes=[pltpu.VMEM((3, N_EXPERTS * COMB_WIN * ROW_TILE, LANES), F32),
                            pltpu.SemaphoreType.DMA((3,))],
        ),
        out_shape=jax.ShapeDtypeStruct((n, D_MODEL), F32),
        compiler_params=_cparams(("arbitrary",), 40),
        name="expert_combine",
    )(cum, nrounds, x2, aff, pos_t, ye)


def _expert_choice(x2, hn, aff, wgu, wd):
    n = x2.shape[0]
    cap = EC_CAPACITY * n // N_EXPERTS
    aff_t = aff.T
    thr, need = _thresh(aff_t, cap)
    pos = _slots(aff_t, thr, need)
    cnt = jnp.sum((pos >= 0).reshape(N_EXPERTS, n // TOK_BLK, TOK_BLK), axis=-1, dtype=I32)
    cum = jnp.concatenate([jnp.zeros((N_EXPERTS, 1), I32), jnp.cumsum(cnt, axis=1, dtype=I32)], axis=1)
    nrounds = jnp.maximum(1, (jnp.max(cnt, axis=0) + COMB_WIN - 1) // COMB_WIN).astype(I32)
    xe = _gather(cum, nrounds, hn, pos, cap)
    ye = _ffn(xe, wgu, wd, cap)
    return _combine(cum, nrounds, x2, aff, pos.T, ye, cap)


def _prep_layer(l, p, lb):
    row = lambda a: a.astype(F32).reshape(1, -1)
    qkg = jnp.concatenate([jnp.tile(p["na_q_norm"][l] * (NA_HD ** -0.5), NA_HEADS),
                           jnp.tile(p["na_k_norm"][l], NA_HEADS)]).reshape(1, 2 * NA_WIDTH)
    idx = jnp.arange(MXU_DIM) // NA_HD
    pad = D_FF_PAD - D_FF
    wgu = p["w_gate_up"][l]
    zcol = jnp.zeros((N_EXPERTS, D_MODEL, pad), BF16)
    return dict(
        norm_mix=row(p["norm_mix"][l]), w_in=p["w_in"][l].astype(BF16),
        grp=(idx[:, None] == idx[None, :]).astype(BF16), qkg=qkg.astype(F32),
        bias_tbl=_na_bias_table(p["na_rel_bias"][l]),
        lb4=lb[:, l].reshape(2, HG_HEADS, 1, HG_DK),
        conv_w=p["conv_w"][l].astype(F32), conv_b=row(p["conv_b"][l]),
        ln_g=row(p["conv_ln_g"][l]), ln_b=row(p["conv_ln_b"][l]),
        og=jnp.tile(p["hg_out_norm"][l], HG_HEADS).reshape(1, -1).astype(F32),
        wna=p["w_na_br"][l].astype(BF16), whg=p["w_hg_br"][l].astype(BF16), wcv=p["w_cv_br"][l].astype(BF16),
        wout=p["w_out"][l].astype(BF16),
        norm_mem=row(p["norm_mem"][l]), mem_kv_norm=row(p["mem_kv_norm"][l]),
        wq=p["wq_mem"][l].astype(BF16), wkv=p["wkv_mem"][l].astype(BF16),
        qn=row(p["mem_q_norm"][l]), kn=row(p["mem_k_norm"][l]), wo=p["wo_mem"][l].astype(BF16),
        norm_ffn=row(p["norm_ffn"][l]), w_router=p["w_router"][l].astype(F32),
        wgu=jnp.concatenate([wgu[:, :, :D_FF].astype(BF16), zcol, wgu[:, :, D_FF:].astype(BF16), zcol], axis=2),
        wd=jnp.concatenate([p["w_down"][l].astype(BF16), jnp.zeros((N_EXPERTS, pad, D_MODEL), BF16)], axis=1),
    )


def _layer(x3, mem3, w):
    b, l, _ = x3.shape
    n = b * l
    x2 = x3.reshape(n, D_MODEL)
    proj2 = _inproj(x2, w["norm_mix"], w["w_in"], w["grp"], w["qkg"])
    proj3 = proj2.reshape(b, l, D_IN)
    na = _na(proj3, w["bias_tbl"])
    o_f, o_b = _hgrn(proj3, w["lb4"])
    cv = _conv(proj3, w["conv_w"], w["conv_b"], w["ln_g"], w["ln_b"])
    x2 = _merge(na.reshape(n, -1), o_f.reshape(n, -1), o_b.reshape(n, -1), proj2, cv.reshape(n, -1), x2,
                w["og"], w["wna"], w["whg"], w["wcv"], w["wout"])
    kv = _memkv(mem3.reshape(-1, D_MODEL), w["mem_kv_norm"], w["wkv"], w["kn"])
    x3, hn, aff = _memattn(x2.reshape(b, l, D_MODEL), w["norm_mem"], w["wq"], w["qn"],
                           kv.reshape(b, -1, 2 * MEM_WIDTH), w["wo"], w["norm_ffn"], w["w_router"])
    x2 = _expert_choice(x3.reshape(n, D_MODEL), hn.reshape(n, D_MODEL), aff.reshape(n, N_EXPERTS),
                        w["wgu"], w["wd"])
    return x2.reshape(b, l, D_MODEL)


def kernel(x_prompt, x_sample, mem_prompt, mem_sample, norm_mix, w_in, na_q_norm, na_k_norm, na_rel_bias, w_na_br, hg_lb, hg_out_norm, w_hg_br, conv_w, conv_b, conv_ln_g, conv_ln_b, w_cv_br, w_out, norm_mem, mem_kv_norm, wq_mem, wkv_mem, mem_q_norm, mem_k_norm, wo_mem, norm_ffn, w_router, w_gate_up, w_down):
    p = dict(norm_mix=norm_mix, w_in=w_in, na_q_norm=na_q_norm, na_k_norm=na_k_norm, na_rel_bias=na_rel_bias,
             w_na_br=w_na_br, hg_out_norm=hg_out_norm, w_hg_br=w_hg_br, conv_w=conv_w, conv_b=conv_b,
             conv_ln_g=conv_ln_g, conv_ln_b=conv_ln_b, w_cv_br=w_cv_br, w_out=w_out, norm_mem=norm_mem,
             mem_kv_norm=mem_kv_norm, wq_mem=wq_mem, wkv_mem=wkv_mem, mem_q_norm=mem_q_norm,
             mem_k_norm=mem_k_norm, wo_mem=wo_mem, norm_ffn=norm_ffn, w_router=w_router, w_gate_up=w_gate_up,
             w_down=w_down)
    depth = w_in.shape[0]
    sm = jax.nn.softmax(hg_lb.astype(F32), axis=1)
    lb = jnp.cumsum(sm, axis=1) - sm[:, :1]
    xs = [x_prompt, x_sample]
    mems = [mem_prompt, mem_sample]
    for l in range(depth):
        w = _prep_layer(l, p, lb)
        xs = [_layer(x, m, w) for x, m in zip(xs, mems)]
    return tuple(xs)
```

```python
import functools

import jax
import jax.numpy as jnp
import numpy as np
from jax import lax
from jax.experimental import pallas as pl
from jax.experimental.pallas import tpu as pltpu

F32 = jnp.float32
BF16 = jnp.bfloat16
I32 = jnp.int32

D_MODEL = 1024
GRID_W = 64
NA_HEADS = 8
NA_HD = 64
NA_WIDTH = NA_HEADS * NA_HD
WIN_R = 8
WIN_C = 16
HG_HEADS = 4
HG_DK = 128
HG_DV = 128
HG_KW = HG_HEADS * HG_DK
F_MIN = 1e-20
D_CONV = 512
CONV_K = 31
MEM_HEADS = 4
MEM_HD = 128
MEM_WIDTH = MEM_HEADS * MEM_HD
N_EXPERTS = 16
EC_CAPACITY = 2
D_FF = 2752
EPS = 1e-6
D_IN = 8192

COL_NA_Q, COL_NA_K, COL_NA_V = 0, 1, 2
COL_HG_Q, COL_HG_FF, COL_HG_FB, COL_HG_V, COL_HG_G = 3, 4, 5, 6, 7
COL_CV_A, COL_CV_B = 8, 9
COL_G_NA, COL_G_HG, COL_G_CV = 5, 6, 7

LANES = 128
SUBLANES = 8
MXU_DIM = 256
D_FF_PAD = 2816
NEG_BIG = -1e30
LOG2E = 1.4426950408889634

HG_CHUNK = 256
TOK_BLK = 256
ROW_TILE = D_MODEL // LANES
COMB_WIN = 64


def _cparams(sem, vmem_mb):
    return pltpu.CompilerParams(dimension_semantics=sem, vmem_limit_bytes=vmem_mb << 20)


def _sigmoid_pair(x):
    t = 0.5 * jnp.tanh(0.5 * x)
    return 0.5 + t, 0.5 - t


def _sigmoid(x):
    return 0.5 + 0.5 * jnp.tanh(0.5 * x)


def _neg_abs(x):
    return lax.bitcast_convert_type(lax.bitcast_convert_type(x, I32) | jnp.int32(-2 ** 31), F32)


def _split2(a):
    hi = a.astype(BF16)
    lo = (a - hi.astype(F32)).astype(BF16)
    return hi, lo


def _dot(a, b):
    return jnp.dot(a, b, preferred_element_type=F32)


def _dot_nt(a, b):
    return lax.dot_general(a, b, (((1,), (1,)), ((), ())), preferred_element_type=F32)


def _dot_tn(a, b):
    return lax.dot_general(a, b, (((0,), (0,)), ((), ())), preferred_element_type=F32)


def _rms_rows(x, g):
    return x * lax.rsqrt(jnp.mean(x * x, axis=-1, keepdims=True) + EPS) * g


def _inproj_body(x_ref, g_ref, w_ref, grp_ref, qkg_ref, o_ref, h_ref):
    j = pl.program_id(1)

    @pl.when(j == 0)
    def _():
        h_ref[...] = _rms_rows(x_ref[...], g_ref[...]).astype(BF16)

    acc = _dot(h_ref[...], w_ref[...])

    @pl.when(j == 0)
    def _():
        nqk = qkg_ref.shape[1]
        qk = acc[:, :nqk]
        sq = (qk * qk).astype(BF16)
        w = grp_ref.shape[0]
        ss = jnp.concatenate([_dot(sq[:, c * w:(c + 1) * w], grp_ref[...]) for c in range(nqk // w)], axis=1)
        o_ref[:, :nqk] = (qk * lax.rsqrt(ss * (1.0 / NA_HD) + EPS) * qkg_ref[...]).astype(BF16)
        o_ref[:, nqk:] = acc[:, nqk:].astype(BF16)

    @pl.when(j != 0)
    def _():
        o_ref[...] = acc.astype(BF16)


def _inproj(x2, g, w_bf, grp, qkg, tm=1024, tn=2048):
    n = x2.shape[0]
    assert tn > qkg.shape[1]
    return pl.pallas_call(
        _inproj_body,
        grid=(n // tm, D_IN // tn),
        in_specs=[
            pl.BlockSpec((tm, D_MODEL), lambda i, j: (i, 0)),
            pl.BlockSpec((1, D_MODEL), lambda i, j: (0, 0)),
            pl.BlockSpec((D_MODEL, tn), lambda i, j: (0, j)),
            pl.BlockSpec((MXU_DIM, MXU_DIM), lambda i, j: (0, 0)),
            pl.BlockSpec((1, 2 * NA_WIDTH), lambda i, j: (0, 0)),
        ],
        out_specs=pl.BlockSpec((tm, tn), lambda i, j: (i, j)),
        out_shape=jax.ShapeDtypeStruct((n, D_IN), BF16),
        scratch_shapes=[pltpu.VMEM((tm, D_MODEL), BF16)],
        compiler_params=_cparams(("parallel", "arbitrary"), 48),
        name="in_proj",
    )(x2, g, w_bf, grp, qkg)


NA_QROWS = 4
NA_KROWS = NA_QROWS + WIN_R


def _na_body(q_ref, k_ref, v_ref, bias_ref, o_ref):
    q = q_ref[...]
    k = k_ref[0]
    v = v_ref[0]
    outs = []
    for h in range(NA_HEADS):
        sl = slice(h * NA_HD, (h + 1) * NA_HD)
        s = _dot_nt(q[:, sl], k[:, sl]) + bias_ref[h]
        m = jnp.max(s, axis=-1, keepdims=True)
        p = jnp.exp(s - m)
        l = jnp.sum(p, axis=-1, keepdims=True)
        outs.append(_dot(p.astype(BF16), v[:, sl]) / l)
    o_ref[...] = jnp.concatenate(outs, axis=1).astype(BF16)


def _na(proj3, bias_tbl):
    b, l, _ = proj3.shape
    rows = l // GRID_W
    assert rows % NA_QROWS == 0 and rows >= 2 * WIN_R
    nblk = rows // NA_QROWS

    def key_row0(g):
        return jnp.clip(NA_QROWS * g - WIN_R // 2, 0, rows - NA_KROWS)

    def variant(g):
        return jnp.where(g == 0, 0, jnp.where(g == nblk - 1, 2, 1))

    qblk = (None, NA_QROWS * GRID_W, NA_WIDTH)
    kblk = (pl.Element(1), pl.Element(NA_KROWS * GRID_W), pl.Element(NA_WIDTH))
    return pl.pallas_call(
        _na_body,
        grid=(b, nblk),
        in_specs=[pl.BlockSpec(qblk, lambda bi, g: (bi, g, COL_NA_Q)),
                  pl.BlockSpec(kblk, lambda bi, g: (bi, key_row0(g) * GRID_W, COL_NA_K * NA_WIDTH)),
                  pl.BlockSpec(kblk, lambda bi, g: (bi, key_row0(g) * GRID_W, COL_NA_V * NA_WIDTH)),
                  pl.BlockSpec((None, NA_HEADS, NA_QROWS * GRID_W, NA_KROWS * GRID_W),
                               lambda bi, g: (variant(g), 0, 0, 0))],
        out_specs=pl.BlockSpec(qblk, lambda bi, g: (bi, g, 0)),
        out_shape=jax.ShapeDtypeStruct((b, l, NA_WIDTH), BF16),
        compiler_params=_cparams(("parallel", "arbitrary"), 48),
        name="na_attention",
    )(proj3, proj3, proj3, bias_tbl)


def _na_bias_table(rel_bias):
    c = np.arange(GRID_W)
    c0 = np.clip(c - WIN_C // 2, 0, GRID_W - WIN_C)
    kc = np.arange(GRID_W)
    col_ok = (kc[None, :] >= c0[:, None]) & (kc[None, :] < c0[:, None] + WIN_C)
    dcol = kc[None, :] - c[:, None] + WIN_C - 1
    pick = (dcol[:, :, None] == np.arange(2 * WIN_C - 1)[None, None, :]) & col_ok[:, :, None]
    toe = jnp.einsum('hrd,ckd->hrck', rel_bias.astype(F32), jnp.asarray(pick, F32),
                     precision=lax.Precision.HIGHEST)
    toe = jnp.where(jnp.asarray(col_ok)[None, None], toe, NEG_BIG)
    pad = NA_KROWS
    toe = jnp.pad(toe, ((0, 0), (pad, pad), (0, 0), (0, 0)), constant_values=NEG_BIG)
    i = np.arange(NA_KROWS)
    variants = []
    for var in range(3):
        per_row = []
        for rho in range(NA_QROWS):
            off = {0: rho, 1: rho + WIN_R // 2, 2: rho + WIN_R}[var]
            lo = {0: 0, 1: rho, 2: NA_QROWS}[var]
            row_ok = (i >= lo) & (i < lo + WIN_R)
            dr0 = -off + WIN_R - 1 + pad
            t = toe[:, dr0:dr0 + NA_KROWS]
            t = jnp.where(jnp.asarray(row_ok)[None, :, None, None], t, NEG_BIG)
            per_row.append(t.transpose(0, 2, 1, 3).reshape(NA_HEADS, GRID_W, NA_KROWS * GRID_W))
        variants.append(jnp.concatenate(per_row, axis=1))
    return jnp.stack(variants)


def _hgrn_chunk(hq, hf, v, lb, st_ref, rev):
    c = hq.shape[0]
    hq = hq.astype(F32)
    q = hq * _sigmoid(hq) * (HG_DK ** -0.5)
    s_pos, s_neg = _sigmoid_pair(hf.astype(F32))
    f = lb + (1.0 - lb) * s_pos
    logf = jnp.log(jnp.maximum(f, F_MIN))
    k = (1.0 - lb) * s_neg

    hb = c // 2
    t_col = lax.broadcasted_iota(I32, (c, 1), 0)
    t_mat = lax.broadcasted_iota(I32, (c, c), 0)
    s_mat = lax.broadcasted_iota(I32, (c, c), 1)
    tri = (s_mat >= t_mat) if rev else (s_mat <= t_mat)
    hi, lo = _split2(logf)
    cs = _dot(tri.astype(BF16), jnp.concatenate([hi, lo], axis=1))
    a = (cs[:, :HG_DK] + cs[:, HG_DK:]) * LOG2E

    st = st_ref[...]
    o_state = _dot_nt((q * jnp.exp2(a)).astype(BF16), st.astype(BF16))

    tl = lax.broadcasted_iota(I32, (hb, hb), 0)
    sl = lax.broadcasted_iota(I32, (hb, hb), 1)
    x = tl ^ sl
    keep = (tl <= sl) if rev else (tl >= sl)
    dg = jnp.sum(q * k, axis=1, keepdims=True)
    att = [jnp.where(x == 0, dg[b * hb:(b + 1) * hb], 0.0) for b in range(2)]
    a8 = a.reshape(c // 8, 8, HG_DK)
    sub = lax.broadcasted_iota(I32, (1, 8, 1), 1)
    m = 1
    while m < hb:
        if m >= 4:
            ar = a.reshape(c // (2 * m), 2 * m, HG_DK)
            row = m if rev else m - 1
            ab = jnp.broadcast_to(ar[:, row:row + 1, :], ar.shape).reshape(c, HG_DK)
        else:
            off = m if rev else m - 1
            rows = [a8[:, p + off:p + off + 1, :] for p in range(0, 8, 2 * m)]
            ab = rows[-1]
            for idx in range(len(rows) - 2, -1, -1):
                ab = jnp.where(sub < (idx + 1) * 2 * m, rows[idx], ab)
            ab = jnp.broadcast_to(ab, a8.shape).reshape(c, HG_DK)
        e = jnp.exp2(_neg_abs(a - ab))
        upper = (t_col & m) != 0
        q_half = jnp.logical_not(upper) if rev else upper
        z = (jnp.where(q_half, q, k) * e).astype(BF16)
        for b in range(2):
            zb = z[b * hb:(b + 1) * hb]
            att[b] = jnp.where(x >= m, _dot_nt(zb, zb), att[b])
        m *= 2
    att = [jnp.where(keep, t, 0.0).astype(BF16) for t in att]

    qs, ks = (slice(0, hb), slice(hb, c)) if rev else (slice(hb, c), slice(0, hb))
    ab = a[hb:hb + 1] if rev else a[hb - 1:hb]
    qe = (q[qs] * jnp.exp2(_neg_abs(a[qs] - ab))).astype(BF16)
    ke = (k[ks] * jnp.exp2(_neg_abs(a[ks] - ab))).astype(BF16)
    cross = _dot(_dot_nt(qe, ke).astype(BF16), v[ks])
    o0 = _dot(att[0], v[:hb])
    o1 = _dot(att[1], v[hb:])
    o = o_state + (jnp.concatenate([o0 + cross, o1], axis=0) if rev else jnp.concatenate([o0, o1 + cross], axis=0))

    a_end = a[0:1, :] if rev else a[c - 1:c, :]
    k_end = (k * jnp.exp2(a_end - a)).astype(BF16)
    st_ref[...] = st * jnp.exp2(a_end) + _dot_tn(v, k_end)
    return o


def _hgrn_body(qf_ref, ff_ref, vf_ref, qb_ref, fb_ref, vb_ref, lb_ref, of_ref, ob_ref, st_ref):
    @pl.when(pl.program_id(1) == 0)
    def _():
        st_ref[...] = jnp.zeros_like(st_ref)

    for h in range(HG_HEADS):
        sl = slice(h * HG_DK, (h + 1) * HG_DK)
        of_ref[:, sl] = _hgrn_chunk(qf_ref[:, sl], ff_ref[:, sl], vf_ref[:, sl], lb_ref[0, h], st_ref.at[0, h], False)
        ob_ref[:, sl] = _hgrn_chunk(qb_ref[:, sl], fb_ref[:, sl], vb_ref[:, sl], lb_ref[1, h], st_ref.at[1, h], True)


def _hgrn(proj3, lb4, chunk=HG_CHUNK):
    b, l, _ = proj3.shape
    n = l // chunk
    width = HG_HEADS * HG_DK

    def spec(col, back):
        if back:
            return pl.BlockSpec((None, chunk, width), lambda bi, i: (bi, n - 1 - i, col))
        return pl.BlockSpec((None, chunk, width), lambda bi, i: (bi, i, col))

    return pl.pallas_call(
        _hgrn_body,
        grid=(b, n),
        in_specs=[spec(COL_HG_Q, False), spec(COL_HG_FF, False), spec(COL_HG_V, False),
                  spec(COL_HG_Q, True), spec(COL_HG_FB, True), spec(COL_HG_V, True),
                  pl.BlockSpec((2, HG_HEADS, 1, HG_DK), lambda bi, i: (0, 0, 0, 0))],
        out_specs=[spec(0, False), spec(0, True)],
        out_shape=[jax.ShapeDtypeStruct((b, l, HG_HEADS * HG_DV), F32)] * 2,
        scratch_shapes=[pltpu.VMEM((2, HG_HEADS, HG_DV, HG_DK), F32)],
        compiler_params=_cparams(("parallel", "arbitrary"), 32),
        name="hgrn2",
    )(*([proj3] * 6), lb4)


CONV_HALO = 16


def _conv_body(a_ref, b_ref, ap_ref, bp_ref, an_ref, bn_ref, w_ref, cb_ref, g_ref, be_ref, o_ref, u_ref, sh_ref):
    i = pl.program_id(1)
    n = pl.num_programs(1)
    t = a_ref.shape[0]

    def glu(a, b):
        return a[...].astype(F32) * _sigmoid(b[...].astype(F32))

    u_ref[pl.ds(0, CONV_HALO), :] = jnp.where(i > 0, glu(ap_ref, bp_ref), 0.0)
    u_ref[pl.ds(CONV_HALO, t), :] = glu(a_ref, b_ref)
    u_ref[pl.ds(CONV_HALO + t, CONV_HALO), :] = jnp.where(i < n - 1, glu(an_ref, bn_ref), 0.0)
    span = t + 2 * CONV_HALO - SUBLANES
    for r in range(1, SUBLANES):
        sh_ref[r - 1] = u_ref[pl.ds(r, span), :]
    base = CONV_HALO - CONV_K // 2
    acc = jnp.zeros((t, D_CONV), F32)
    for kk in range(CONV_K):
        off = base + kk
        r = off % SUBLANES
        win = u_ref[pl.ds(off, t), :] if r == 0 else sh_ref[r - 1, pl.ds(off - r, t), :]
        acc = acc + w_ref[kk:kk + 1, :] * win
    y = acc + cb_ref[...]
    mu = jnp.mean(y, axis=-1, keepdims=True)
    yc = y - mu
    var = jnp.mean(yc * yc, axis=-1, keepdims=True)
    yn = yc * lax.rsqrt(var + EPS) * g_ref[...] + be_ref[...]
    o_ref[...] = (yn * _sigmoid(yn)).astype(BF16)


def _conv(proj3, conv_w, conv_b, ln_g, ln_b, t=512):
    b, l, _ = proj3.shape
    nt = l // t
    hb = t // CONV_HALO
    nh = l // CONV_HALO
    main = lambda col: pl.BlockSpec((None, t, D_CONV), lambda bi, i: (bi, i, col))
    prev = lambda col: pl.BlockSpec((None, CONV_HALO, D_CONV), lambda bi, i: (bi, jnp.maximum(i * hb - 1, 0), col))
    nxt = lambda col: pl.BlockSpec((None, CONV_HALO, D_CONV), lambda bi, i: (bi, jnp.minimum((i + 1) * hb, nh - 1), col))
    vec = pl.BlockSpec((1, D_CONV), lambda bi, i: (0, 0))
    return pl.pallas_call(
        _conv_body,
        grid=(b, nt),
        in_specs=[main(COL_CV_A), main(COL_CV_B), prev(COL_CV_A), prev(COL_CV_B), nxt(COL_CV_A), nxt(COL_CV_B),
                  pl.BlockSpec((CONV_K, D_CONV), lambda bi, i: (0, 0)), vec, vec, vec],
        out_specs=pl.BlockSpec((None, t, D_CONV), lambda bi, i: (bi, i, 0)),
        out_shape=jax.ShapeDtypeStruct((b, l, D_CONV), BF16),
        scratch_shapes=[pltpu.VMEM((t + 2 * CONV_HALO, D_CONV), F32),
                        pltpu.VMEM((SUBLANES - 1, t + 2 * CONV_HALO - SUBLANES, D_CONV), F32)],
        compiler_params=_cparams(("parallel", "arbitrary"), 32),
        name="conv_module",
    )(*([proj3] * 6), conv_w, conv_b, ln_g, ln_b)


def _merge_body(na_ref, of_ref, ob_ref, hg_ref, cv_ref, gna_ref, ghg_ref, gcv_ref, x_ref,
                og_ref, wna_ref, whg_ref, wcv_ref, wout_ref, o_ref):
    o = of_ref[...] + ob_ref[...]
    parts = []
    for h in range(HG_HEADS):
        oh = o[:, h * HG_DV:(h + 1) * HG_DV]
        parts.append(oh * lax.rsqrt(jnp.mean(oh * oh, axis=-1, keepdims=True) + EPS))
    gg = hg_ref[...].astype(F32)
    hg = jnp.concatenate(parts, axis=1) * og_ref[...] * (gg * _sigmoid(gg))
    merged = (_sigmoid(gna_ref[...].astype(F32)) * _dot(na_ref[...], wna_ref[...])
              + _sigmoid(ghg_ref[...].astype(F32)) * _dot(hg.astype(BF16), whg_ref[...])
              + _sigmoid(gcv_ref[...].astype(F32)) * _dot(cv_ref[...], wcv_ref[...]))
    o_ref[...] = x_ref[...] + _dot(merged.astype(BF16), wout_ref[...])


def _merge(na2, of2, ob2, proj2, cv2, x2, og, wna, whg, wcv, wout, t=512):
    n = x2.shape[0]
    half = lambda col=0: pl.BlockSpec((t, 512), lambda i: (i, col))
    full = lambda col=0: pl.BlockSpec((t, D_MODEL), lambda i: (i, col))
    wspec = lambda r: pl.BlockSpec((r, D_MODEL), lambda i: (0, 0))
    return pl.pallas_call(
        _merge_body,
        grid=(n // t,),
        in_specs=[half(), half(), half(), half(COL_HG_G), half(), full(COL_G_NA), full(COL_G_HG), full(COL_G_CV),
                  full(), pl.BlockSpec((1, 512), lambda i: (0, 0)), wspec(512), wspec(512), wspec(512),
                  wspec(D_MODEL)],
        out_specs=full(),
        out_shape=jax.ShapeDtypeStruct((n, D_MODEL), F32),
        compiler_params=_cparams(("parallel",), 48),
        name="merge_out_proj",
    )(na2, of2, ob2, proj2, cv2, proj2, proj2, proj2, x2, og, wna, whg, wcv, wout)


def _memkv_body(m_ref, g_ref, w_ref, kn_ref, o_ref):
    kv = _dot(_rms_rows(m_ref[...], g_ref[...]).astype(BF16), w_ref[...])
    parts = []
    for h in range(MEM_HEADS):
        kh = kv[:, h * MEM_HD:(h + 1) * MEM_HD]
        parts.append(_rms_rows(kh, kn_ref[...]))
    parts.append(kv[:, MEM_WIDTH:])
    o_ref[...] = jnp.concatenate(parts, axis=1).astype(BF16)


def _memkv(mem2, g, wkv, kn, t=256):
    n = mem2.shape[0]
    return pl.pallas_call(
        _memkv_body,
        grid=(n // t,),
        in_specs=[pl.BlockSpec((t, D_MODEL), lambda i: (i, 0)), pl.BlockSpec((1, D_MODEL), lambda i: (0, 0)),
                  pl.BlockSpec((D_MODEL, 2 * MEM_WIDTH), lambda i: (0, 0)),
                  pl.BlockSpec((1, MEM_HD), lambda i: (0, 0))],
        out_specs=pl.BlockSpec((t, 2 * MEM_WIDTH), lambda i: (i, 0)),
        out_shape=jax.ShapeDtypeStruct((n, 2 * MEM_WIDTH), BF16),
        compiler_params=_cparams(("parallel",), 32),
        name="mem_kv",
    )(mem2, g, wkv, kn)


def _memattn_body(x_ref, g_ref, wq_ref, qn_ref, kv_ref, wo_ref, o_ref):
    x = x_ref[...]
    q = _dot(_rms_rows(x, g_ref[...]).astype(BF16), wq_ref[...])
    kv = kv_ref[...]
    outs = []
    for h in range(MEM_HEADS):
        sl = slice(h * MEM_HD, (h + 1) * MEM_HD)
        qh = _rms_rows(q[:, sl], qn_ref[...]) * (MEM_HD ** -0.5)
        s = _dot_nt(qh.astype(BF16), kv[:, sl])
        m = jnp.max(s, axis=-1, keepdims=True)
        p = jnp.exp(s - m)
        l = jnp.sum(p, axis=-1, keepdims=True)
        outs.append(_dot(p.astype(BF16), kv[:, MEM_WIDTH + h * MEM_HD:MEM_WIDTH + (h + 1) * MEM_HD]) / l)
    o_ref[...] = x + _dot(jnp.concatenate(outs, axis=1).astype(BF16), wo_ref[...])


def _memattn(x3, g, wq, qn, kv3, wo, t=512):
    b, l, _ = x3.shape
    m = kv3.shape[1]
    tok = pl.BlockSpec((None, t, D_MODEL), lambda bi, i: (bi, i, 0))
    const = lambda r, c: pl.BlockSpec((r, c), lambda bi, i: (0, 0))
    return pl.pallas_call(
        _memattn_body,
        grid=(b, l // t),
        in_specs=[tok, const(1, D_MODEL), const(D_MODEL, MEM_WIDTH), const(1, MEM_HD),
                  pl.BlockSpec((None, m, 2 * MEM_WIDTH), lambda bi, i: (bi, 0, 0)), const(MEM_WIDTH, D_MODEL)],
        out_specs=tok,
        out_shape=jax.ShapeDtypeStruct((b, l, D_MODEL), F32),
        compiler_params=_cparams(("parallel", "arbitrary"), 32),
        name="mem_attention",
    )(x3, g, wq, qn, kv3, wo)


def _router_body(x_ref, g_ref, w_ref, h_ref, a_ref):
    hn = _rms_rows(x_ref[...], g_ref[...])
    h_ref[...] = hn.astype(BF16)
    h_hi, h_lo = _split2(hn)
    w_hi, w_lo = _split2(w_ref[...])
    both = _dot(h_hi, jnp.concatenate([w_hi, w_lo], axis=1))
    logits = both[:, :N_EXPERTS] + both[:, N_EXPERTS:] + _dot(h_lo, w_hi)
    e = jnp.exp(logits - jnp.max(logits, axis=-1, keepdims=True))
    a_ref[...] = e / jnp.sum(e, axis=-1, keepdims=True)


def _router(x2, g, w_router, t=512):
    n = x2.shape[0]
    return pl.pallas_call(
        _router_body,
        grid=(n // t,),
        in_specs=[pl.BlockSpec((t, D_MODEL), lambda i: (i, 0)), pl.BlockSpec((1, D_MODEL), lambda i: (0, 0)),
                  pl.BlockSpec((D_MODEL, N_EXPERTS), lambda i: (0, 0))],
        out_specs=[pl.BlockSpec((t, D_MODEL), lambda i: (i, 0)), pl.BlockSpec((t, N_EXPERTS), lambda i: (i, 0))],
        out_shape=[jax.ShapeDtypeStruct((n, D_MODEL), BF16), jax.ShapeDtypeStruct((n, N_EXPERTS), F32)],
        compiler_params=_cparams(("parallel",), 32),
        name="router",
    )(x2, g, w_router)


def _thresh_body(a_ref, thr_ref, need_ref, *, cap):
    bits = pltpu.bitcast(a_ref[...], I32)

    def step(b, thr):
        cand = thr | (1 << (30 - b))
        cnt = jnp.sum((bits >= cand).astype(I32), axis=1, keepdims=True)
        return jnp.where(cnt >= cap, cand, thr)

    thr = lax.fori_loop(0, 31, step, jnp.zeros((N_EXPERTS, 1), I32))
    thr_ref[...] = thr
    need_ref[...] = cap - jnp.sum((bits > thr).astype(I32), axis=1, keepdims=True)


def _thresh(aff_t, cap):
    n = aff_t.shape[1]
    return pl.pallas_call(
        functools.partial(_thresh_body, cap=cap),
        grid=(1,),
        in_specs=[pl.BlockSpec((N_EXPERTS, n), lambda i: (0, 0))],
        out_specs=[pl.BlockSpec((N_EXPERTS, 1), lambda i: (0, 0))] * 2,
        out_shape=[jax.ShapeDtypeStruct((N_EXPERTS, 1), I32)] * 2,
        compiler_params=_cparams(("arbitrary",), 32),
        name="topk_threshold",
    )(aff_t)


def _slots_body(a_ref, thr_ref, need_ref, pos_ref, eq_off, sel_off):
    @pl.when(pl.program_id(0) == 0)
    def _():
        eq_off[...] = jnp.zeros_like(eq_off)
        sel_off[...] = jnp.zeros_like(sel_off)

    bits = pltpu.bitcast(a_ref[...], I32)
    w = bits.shape[1]
    thr = thr_ref[...]
    before = (lax.broadcasted_iota(I32, (w, w), 0) < lax.broadcasted_iota(I32, (w, w), 1)).astype(BF16)
    eq = bits == thr
    eq_f = eq.astype(F32)
    eq_rank = _dot(eq_f.astype(BF16), before) + eq_off[...]
    sel = (bits > thr) | (eq & (eq_rank < need_ref[...].astype(F32)))
    sel_f = sel.astype(F32)
    slot = _dot(sel_f.astype(BF16), before) + sel_off[...]
    pos_ref[...] = jnp.where(sel, slot, -1.0).astype(I32)
    eq_off[...] += jnp.sum(eq_f, axis=1, keepdims=True)
    sel_off[...] += jnp.sum(sel_f, axis=1, keepdims=True)


def _slots(aff_t, thr, need, w=TOK_BLK):
    n = aff_t.shape[1]
    col = pl.BlockSpec((N_EXPERTS, 1), lambda i: (0, 0))
    return pl.pallas_call(
        _slots_body,
        grid=(n // w,),
        in_specs=[pl.BlockSpec((N_EXPERTS, w), lambda i: (0, i)), col, col],
        out_specs=pl.BlockSpec((N_EXPERTS, w), lambda i: (0, i)),
        out_shape=jax.ShapeDtypeStruct((N_EXPERTS, n), I32),
        scratch_shapes=[pltpu.VMEM((N_EXPERTS, 1), F32)] * 2,
        compiler_params=_cparams(("arbitrary",), 32),
        name="topk_slots",
    )(aff_t, thr, need)


def _gather_body(cum_ref, nr_ref, h_ref, pos_ref, xe_hbm, stg, sem, *, cap):
    i = pl.program_id(0)
    nblk = pl.num_programs(0)
    win_rows = COMB_WIN * ROW_TILE

    def copies(blk, r, buf):
        return [pltpu.make_async_copy(
            stg.at[buf, pl.ds(e * win_rows, win_rows), :],
            xe_hbm.at[e, pl.ds(pl.multiple_of((cum_ref[e, blk] + r * COMB_WIN) * ROW_TILE, ROW_TILE), win_rows), :],
            sem.at[buf]) for e in range(N_EXPERTS)]

    w_iota = lax.broadcasted_iota(I32, (COMB_WIN, TOK_BLK), 0)

    def stage(r, buf):
        sel = []
        for e in range(N_EXPERTS):
            nominal = cum_ref[e, i] + r * COMB_WIN
            p = pos_ref[e:e + 1, :]
            sel.append((w_iota == jnp.where(p >= nominal, p - nominal, -1)).astype(BF16))
        res = _dot(jnp.concatenate(sel, axis=0), h_ref[...])
        for j in range(ROW_TILE):
            stg[buf, pl.ds(j, N_EXPERTS * COMB_WIN, stride=ROW_TILE), :] = res[:, j * LANES:(j + 1) * LANES]

    @pl.when(i == 0)
    def _():
        stg[2, pl.ds(0, win_rows), :] = jnp.zeros((win_rows, LANES), F32)
        pads = [pltpu.make_async_copy(stg.at[2, pl.ds(0, win_rows), :],
                                      xe_hbm.at[e, pl.ds(cap * ROW_TILE, win_rows), :], sem.at[2])
                for e in range(N_EXPERTS)]
        for c in pads:
            c.start()
        for c in pads:
            c.wait()

    slot = i % 2
    nr = nr_ref[i]
    stage(0, slot)

    @pl.when((i > 0) & (nr_ref[jnp.maximum(i - 1, 0)] == 1))
    def _():
        for c in copies(i - 1, 0, 1 - slot):
            c.wait()

    for c in copies(i, 0, slot):
        c.start()

    @pl.when(nr > 1)
    def _():
        for c in copies(i, 0, slot):
            c.wait()

        def extra(r, carry):
            stage(r, 2)
            for e, c in enumerate(copies(i, r, 2)):
                @pl.when(cum_ref[e, i + 1] > cum_ref[e, i] + r * COMB_WIN)
                def _():
                    c.start()
                    c.wait()
            return carry

        lax.fori_loop(1, nr, extra, 0)

    @pl.when((i == nblk - 1) & (nr == 1))
    def _():
        for c in copies(i, 0, slot):
            c.wait()


def _gather(cum, nrounds, hn, pos, cap):
    n = hn.shape[0]
    return pl.pallas_call(
        functools.partial(_gather_body, cap=cap),
        grid_spec=pltpu.PrefetchScalarGridSpec(
            num_scalar_prefetch=2,
            grid=(n // TOK_BLK,),
            in_specs=[pl.BlockSpec((TOK_BLK, D_MODEL), lambda i, cum, nr: (i, 0)),
                      pl.BlockSpec((N_EXPERTS, TOK_BLK), lambda i, cum, nr: (0, i))],
            out_specs=pl.BlockSpec(memory_space=pl.ANY),
            scratch_shapes=[pltpu.VMEM((3, N_EXPERTS * COMB_WIN * ROW_TILE, LANES), F32),
                            pltpu.SemaphoreType.DMA((3,))],
        ),
        out_shape=jax.ShapeDtypeStruct((N_EXPERTS, (cap + COMB_WIN) * ROW_TILE, LANES), F32),
        compiler_params=_cparams(("arbitrary",), 40),
        name="expert_gather",
    )(cum, nrounds, hn, pos)


def _ffn_body(x_ref, wg_ref, wu_ref, wd_ref, o_ref, acc_ref, xb_ref):
    f = pl.program_id(2)

    @pl.when(f == 0)
    def _():
        tm = xb_ref.shape[0]
        xb_ref[...] = jnp.concatenate([x_ref[pl.ds(j, tm, stride=ROW_TILE), :] for j in range(ROW_TILE)],
                                      axis=1).astype(BF16)

    x = xb_ref[...]
    gate = _dot(x, wg_ref[...])
    up = _dot(x, wu_ref[...])
    part = _dot((gate * _sigmoid(gate) * up).astype(BF16), wd_ref[...])

    last = pl.num_programs(2) - 1

    @pl.when(f == 0)
    def _():
        acc_ref[...] = part

    @pl.when((f != 0) & (f != last))
    def _():
        acc_ref[...] += part

    @pl.when(f == last)
    def _():
        total = acc_ref[...] + part
        tm = total.shape[0]
        for j in range(ROW_TILE):
            o_ref[pl.ds(j, tm, stride=ROW_TILE), :] = total[:, j * LANES:(j + 1) * LANES]


def _ffn(xe, wgu, wd, cap, tm=1024, tf=D_FF_PAD // 2):
    e = xe.shape[0]
    tm = min(tm, cap)
    nf = D_FF_PAD // tf
    assert nf >= 2
    return pl.pallas_call(
        _ffn_body,
        grid=(e, cap // tm, nf),
        in_specs=[pl.BlockSpec((None, tm * ROW_TILE, LANES), lambda ei, i, f: (ei, i, 0)),
                  pl.BlockSpec((None, D_MODEL, tf), lambda ei, i, f: (ei, 0, f)),
                  pl.BlockSpec((None, D_MODEL, tf), lambda ei, i, f: (ei, 0, f + nf)),
                  pl.BlockSpec((None, tf, D_MODEL), lambda ei, i, f: (ei, f, 0))],
        out_specs=pl.BlockSpec((None, tm * ROW_TILE, LANES), lambda ei, i, f: (ei, i, 0)),
        out_shape=jax.ShapeDtypeStruct((e, cap * ROW_TILE, LANES), F32),
        scratch_shapes=[pltpu.VMEM((tm, D_MODEL), F32), pltpu.VMEM((tm, D_MODEL), BF16)],
        compiler_params=_cparams(("parallel", "parallel", "arbitrary"), 56),
        name="expert_ffn",
    )(xe, wgu, wgu, wd)


def _combine_body(cum_ref, nr_ref, x_ref, aff_ref, pos_ref, y_hbm, o_ref, ybuf, sem, *, cap):
    i = pl.program_id(0)
    nblk = pl.num_programs(0)
    win_rows = COMB_WIN * ROW_TILE

    def win_start(blk, e, r):
        return jnp.minimum(cum_ref[e, blk] + r * COMB_WIN, cap - COMB_WIN)

    def copies(blk, r, buf):
        return [pltpu.make_async_copy(
            y_hbm.at[e, pl.ds(pl.multiple_of(win_start(blk, e, r) * ROW_TILE, ROW_TILE), win_rows), :],
            ybuf.at[buf, pl.ds(e * win_rows, win_rows), :],
            sem.at[buf]) for e in range(N_EXPERTS)]

    lane = lax.broadcasted_iota(I32, (TOK_BLK, LANES), 1)
    hi_half = lane >= COMB_WIN
    lane_w = jnp.where(hi_half, lane - COMB_WIN, lane)

    def contribution(r, buf):
        y = jnp.concatenate([ybuf[buf, pl.ds(j, N_EXPERTS * COMB_WIN, stride=ROW_TILE), :]
                             for j in range(ROW_TILE)], axis=1).astype(BF16)
        pos = pos_ref[...]
        aff = aff_ref[...]

        def col(e):
            nominal = cum_ref[e, i] + r * COMB_WIN
            p = pos[:, e:e + 1]
            return jnp.where(p >= nominal, p - win_start(i, e, r), -1)

        parts = []
        for e in range(0, N_EXPERTS, 2):
            d = jnp.where(hi_half, col(e + 1), col(e))
            g = jnp.where(hi_half, aff[:, e + 1:e + 2], aff[:, e:e + 1])
            parts.append(jnp.where(d == lane_w, g, 0.0).astype(BF16))
        return _dot(jnp.concatenate(parts, axis=1), y)

    slot = i % 2

    @pl.when(i == 0)
    def _():
        for c in copies(0, 0, 0):
            c.start()

    @pl.when(i + 1 < nblk)
    def _():
        for c in copies(i + 1, 0, 1 - slot):
            c.start()

    for c in copies(i, 0, slot):
        c.wait()
    o_ref[...] = x_ref[...] + contribution(0, slot)

    def extra(r, carry):
        for c in copies(i, r, 2):
            c.start()
        for c in copies(i, r, 2):
            c.wait()
        o_ref[...] += contribution(r, 2)
        return carry

    lax.fori_loop(1, nr_ref[i], extra, 0)


def _combine(cum, nrounds, x2, aff, pos_t, ye, cap):
    n = x2.shape[0]
    assert 2 * COMB_WIN == LANES and cap >= COMB_WIN
    tok = lambda w: pl.BlockSpec((TOK_BLK, w), lambda i, cum, nr: (i, 0))
    return pl.pallas_call(
        functools.partial(_combine_body, cap=cap),
        grid_spec=pltpu.PrefetchScalarGridSpec(
            num_scalar_prefetch=2,
            grid=(n // TOK_BLK,),
            in_specs=[tok(D_MODEL), tok(N_EXPERTS), tok(N_EXPERTS), pl.BlockSpec(memory_space=pl.ANY)],
            out_specs=tok(D_MODEL),
            scratch_shapes=[pltpu.VMEM((3, N_EXPERTS * COMB_WIN * ROW_TILE, LANES), F32),
                            pltpu.SemaphoreType.DMA((3,))],
        ),
        out_shape=jax.ShapeDtypeStruct((n, D_MODEL), F32),
        compiler_params=_cparams(("arbitrary",), 40),
        name="expert_combine",
    )(cum, nrounds, x2, aff, pos_t, ye)


def _expert_choice(x2, hn, aff, wgu, wd):
    n = x2.shape[0]
    cap = EC_CAPACITY * n // N_EXPERTS
    aff_t = aff.T
    thr, need = _thresh(aff_t, cap)
    pos = _slots(aff_t, thr, need)
    cnt = jnp.sum((pos >= 0).reshape(N_EXPERTS, n // TOK_BLK, TOK_BLK), axis=-1, dtype=I32)
    cum = jnp.concatenate([jnp.zeros((N_EXPERTS, 1), I32), jnp.cumsum(cnt, axis=1, dtype=I32)], axis=1)
    nrounds = jnp.maximum(1, (jnp.max(cnt, axis=0) + COMB_WIN - 1) // COMB_WIN).astype(I32)
    xe = _gather(cum, nrounds, hn, pos, cap)
    ye = _ffn(xe, wgu, wd, cap)
    return _combine(cum, nrounds, x2, aff, pos.T, ye, cap)


def _pad_gate_up_body(x_ref, o_ref):
    x = x_ref[...]
    zeros = jnp.zeros((x.shape[0], D_FF_PAD - D_FF), BF16)
    o_ref[:, 0:D_FF] = x[:, :D_FF].astype(BF16)
    o_ref[:, D_FF:D_FF_PAD] = zeros
    o_ref[:, D_FF_PAD:D_FF_PAD + D_FF] = x[:, D_FF:].astype(BF16)
    o_ref[:, D_FF_PAD + D_FF:] = zeros


def _pad_gate_up(w, t=256):
    e = w.shape[0]
    return pl.pallas_call(
        _pad_gate_up_body,
        grid=(e, D_MODEL // t),
        in_specs=[pl.BlockSpec((None, t, 2 * D_FF), lambda ei, i: (ei, i, 0))],
        out_specs=pl.BlockSpec((None, t, 2 * D_FF_PAD), lambda ei, i: (ei, i, 0)),
        out_shape=jax.ShapeDtypeStruct((e, D_MODEL, 2 * D_FF_PAD), BF16),
        compiler_params=_cparams(("parallel", "parallel"), 32),
        name="pad_gate_up",
    )(w)


def _pad_down_body(x_ref, o_ref):
    x = x_ref[...]
    row = lax.broadcasted_iota(I32, x.shape, 0) + pl.program_id(1) * x.shape[0]
    o_ref[...] = jnp.where(row < D_FF, x, 0.0).astype(BF16)


def _pad_down(w, t=256):
    e = w.shape[0]
    return pl.pallas_call(
        _pad_down_body,
        grid=(e, D_FF_PAD // t),
        in_specs=[pl.BlockSpec((None, t, D_MODEL), lambda ei, i: (ei, i, 0))],
        out_specs=pl.BlockSpec((None, t, D_MODEL), lambda ei, i: (ei, i, 0)),
        out_shape=jax.ShapeDtypeStruct((e, D_FF_PAD, D_MODEL), BF16),
        compiler_params=_cparams(("parallel", "parallel"), 32),
        name="pad_down",
    )(w)


def _prep_layer(l, p, lb):
    row = lambda a: a.astype(F32).reshape(1, -1)
    qkg = jnp.concatenate([jnp.tile(p["na_q_norm"][l] * (NA_HD ** -0.5), NA_HEADS),
                           jnp.tile(p["na_k_norm"][l], NA_HEADS)]).reshape(1, 2 * NA_WIDTH)
    idx = jnp.arange(MXU_DIM) // NA_HD
    return dict(
        norm_mix=row(p["norm_mix"][l]), w_in=p["w_in"][l].astype(BF16),
        grp=(idx[:, None] == idx[None, :]).astype(BF16), qkg=qkg.astype(F32),
        bias_tbl=_na_bias_table(p["na_rel_bias"][l]),
        lb4=lb[:, l].reshape(2, HG_HEADS, 1, HG_DK),
        conv_w=p["conv_w"][l].astype(F32), conv_b=row(p["conv_b"][l]),
        ln_g=row(p["conv_ln_g"][l]), ln_b=row(p["conv_ln_b"][l]),
        og=jnp.tile(p["hg_out_norm"][l], HG_HEADS).reshape(1, -1).astype(F32),
        wna=p["w_na_br"][l].astype(BF16), whg=p["w_hg_br"][l].astype(BF16), wcv=p["w_cv_br"][l].astype(BF16),
        wout=p["w_out"][l].astype(BF16),
        norm_mem=row(p["norm_mem"][l]), mem_kv_norm=row(p["mem_kv_norm"][l]),
        wq=p["wq_mem"][l].astype(BF16), wkv=p["wkv_mem"][l].astype(BF16),
        qn=row(p["mem_q_norm"][l]), kn=row(p["mem_k_norm"][l]), wo=p["wo_mem"][l].astype(BF16),
        norm_ffn=row(p["norm_ffn"][l]), w_router=p["w_router"][l].astype(F32),
        wgu=_pad_gate_up(p["w_gate_up"][l]), wd=_pad_down(p["w_down"][l]),
    )


def _layer(x3, mem3, w):
    b, l, _ = x3.shape
    n = b * l
    x2 = x3.reshape(n, D_MODEL)
    proj2 = _inproj(x2, w["norm_mix"], w["w_in"], w["grp"], w["qkg"])
    proj3 = proj2.reshape(b, l, D_IN)
    na = _na(proj3, w["bias_tbl"])
    o_f, o_b = _hgrn(proj3, w["lb4"])
    cv = _conv(proj3, w["conv_w"], w["conv_b"], w["ln_g"], w["ln_b"])
    x2 = _merge(na.reshape(n, -1), o_f.reshape(n, -1), o_b.reshape(n, -1), proj2, cv.reshape(n, -1), x2,
                w["og"], w["wna"], w["whg"], w["wcv"], w["wout"])
    kv = _memkv(mem3.reshape(-1, D_MODEL), w["mem_kv_norm"], w["wkv"], w["kn"])
    x3 = _memattn(x2.reshape(b, l, D_MODEL), w["norm_mem"], w["wq"], w["qn"],
                  kv.reshape(b, -1, 2 * MEM_WIDTH), w["wo"])
    x2 = x3.reshape(n, D_MODEL)
    hn, aff = _router(x2, w["norm_ffn"], w["w_router"])
    x2 = _expert_choice(x2, hn, aff, w["wgu"], w["wd"])
    return x2.reshape(b, l, D_MODEL)


def kernel(x_prompt, x_sample, mem_prompt, mem_sample, norm_mix, w_in, na_q_norm, na_k_norm, na_rel_bias, w_na_br, hg_lb, hg_out_norm, w_hg_br, conv_w, conv_b, conv_ln_g, conv_ln_b, w_cv_br, w_out, norm_mem, mem_kv_norm, wq_mem, wkv_mem, mem_q_norm, mem_k_norm, wo_mem, norm_ffn, w_router, w_gate_up, w_down):
    p = dict(norm_mix=norm_mix, w_in=w_in, na_q_norm=na_q_norm, na_k_norm=na_k_norm, na_rel_bias=na_rel_bias,
             w_na_br=w_na_br, hg_out_norm=hg_out_norm, w_hg_br=w_hg_br, conv_w=conv_w, conv_b=conv_b,
             conv_ln_g=conv_ln_g, conv_ln_b=conv_ln_b, w_cv_br=w_cv_br, w_out=w_out, norm_mem=norm_mem,
             mem_kv_norm=mem_kv_norm, wq_mem=wq_mem, wkv_mem=wkv_mem, mem_q_norm=mem_q_norm,
             mem_k_norm=mem_k_norm, wo_mem=wo_mem, norm_ffn=norm_ffn, w_router=w_router, w_gate_up=w_gate_up,
             w_down=w_down)
    depth = w_in.shape[0]
    sm = jax.nn.softmax(hg_lb.astype(F32), axis=1)
    lb = jnp.cumsum(sm, axis=1) - sm[:, :1]
    xs = [x_prompt, x_sample]
    mems = [mem_prompt, mem_sample]
    for l in range(depth):
        w = _prep_layer(l, p, lb)
        xs = [_layer(x, m, w) for x, m in zip(xs, mems)]
    return tuple(xs)
```

```python
import functools

import jax
import jax.numpy as jnp
import numpy as np
from jax import lax
from jax.experimental import pallas as pl
from jax.experimental.pallas import tpu as pltpu

F32 = jnp.float32
BF16 = jnp.bfloat16
I32 = jnp.int32

D_MODEL = 1024
GRID_W = 64
NA_HEADS = 8
NA_HD = 64
NA_WIDTH = NA_HEADS * NA_HD
WIN_R = 8
WIN_C = 16
HG_HEADS = 4
HG_DK = 128
HG_DV = 128
HG_KW = HG_HEADS * HG_DK
F_MIN = 1e-20
D_CONV = 512
CONV_K = 31
MEM_HEADS = 4
MEM_HD = 128
MEM_WIDTH = MEM_HEADS * MEM_HD
N_EXPERTS = 16
EC_CAPACITY = 2
D_FF = 2752
EPS = 1e-6
D_IN = 8192

COL_NA_Q, COL_NA_K, COL_NA_V = 0, 1, 2
COL_HG_Q, COL_HG_FF, COL_HG_FB, COL_HG_V, COL_HG_G = 3, 4, 5, 6, 7
COL_CV_A, COL_CV_B = 8, 9
COL_G_NA, COL_G_HG, COL_G_CV = 5, 6, 7

LANES = 128
SUBLANES = 8
MXU_DIM = 256
D_FF_PAD = 2816
NEG_BIG = -1e30
LOG2E = 1.4426950408889634

HG_CHUNK = 256
TOK_BLK = 256
ROW_TILE = D_MODEL // LANES
COMB_WIN = 64


def _cparams(sem, vmem_mb):
    return pltpu.CompilerParams(dimension_semantics=sem, vmem_limit_bytes=vmem_mb << 20)


def _sigmoid_pair(x):
    t = 0.5 * jnp.tanh(0.5 * x)
    return 0.5 + t, 0.5 - t


def _sigmoid(x):
    return 0.5 + 0.5 * jnp.tanh(0.5 * x)


def _neg_abs(x):
    return lax.bitcast_convert_type(lax.bitcast_convert_type(x, I32) | jnp.int32(-2 ** 31), F32)


def _split2(a):
    hi = a.astype(BF16)
    lo = (a - hi.astype(F32)).astype(BF16)
    return hi, lo


def _dot(a, b):
    return jnp.dot(a, b, preferred_element_type=F32)


def _dot_nt(a, b):
    return lax.dot_general(a, b, (((1,), (1,)), ((), ())), preferred_element_type=F32)


def _dot_tn(a, b):
    return lax.dot_general(a, b, (((0,), (0,)), ((), ())), preferred_element_type=F32)


def _rms_rows(x, g):
    return x * lax.rsqrt(jnp.mean(x * x, axis=-1, keepdims=True) + EPS) * g


def _inproj_body(x_ref, g_ref, w_ref, grp_ref, qkg_ref, o_ref, h_ref):
    j = pl.program_id(1)

    @pl.when(j == 0)
    def _():
        h_ref[...] = _rms_rows(x_ref[...], g_ref[...]).astype(BF16)

    acc = _dot(h_ref[...], w_ref[...])

    @pl.when(j == 0)
    def _():
        nqk = qkg_ref.shape[1]
        qk = acc[:, :nqk]
        sq = (qk * qk).astype(BF16)
        w = grp_ref.shape[0]
        ss = jnp.concatenate([_dot(sq[:, c * w:(c + 1) * w], grp_ref[...]) for c in range(nqk // w)], axis=1)
        o_ref[:, :nqk] = (qk * lax.rsqrt(ss * (1.0 / NA_HD) + EPS) * qkg_ref[...]).astype(BF16)
        o_ref[:, nqk:] = acc[:, nqk:].astype(BF16)

    @pl.when(j != 0)
    def _():
        o_ref[...] = acc.astype(BF16)


def _inproj(x2, g, w_bf, grp, qkg, tm=1024, tn=2048):
    n = x2.shape[0]
    assert tn > qkg.shape[1]
    return pl.pallas_call(
        _inproj_body,
        grid=(n // tm, D_IN // tn),
        in_specs=[
            pl.BlockSpec((tm, D_MODEL), lambda i, j: (i, 0)),
            pl.BlockSpec((1, D_MODEL), lambda i, j: (0, 0)),
            pl.BlockSpec((D_MODEL, tn), lambda i, j: (0, j)),
            pl.BlockSpec((MXU_DIM, MXU_DIM), lambda i, j: (0, 0)),
            pl.BlockSpec((1, 2 * NA_WIDTH), lambda i, j: (0, 0)),
        ],
        out_specs=pl.BlockSpec((tm, tn), lambda i, j: (i, j)),
        out_shape=jax.ShapeDtypeStruct((n, D_IN), BF16),
        scratch_shapes=[pltpu.VMEM((tm, D_MODEL), BF16)],
        compiler_params=_cparams(("parallel", "arbitrary"), 48),
        name="in_proj",
    )(x2, g, w_bf, grp, qkg)


NA_QROWS = 4
NA_KROWS = NA_QROWS + WIN_R


def _na_head(q, k, v, bias):
    s = _dot_nt(q, k) + bias
    p = jnp.exp(s - jnp.max(s, axis=-1, keepdims=True))
    return _dot(p.astype(BF16), v) / jnp.sum(p, axis=-1, keepdims=True)


def _na_bias_table(rel_bias):
    c = np.arange(GRID_W)
    c0 = np.clip(c - WIN_C // 2, 0, GRID_W - WIN_C)
    kc = np.arange(GRID_W)
    col_ok = (kc[None, :] >= c0[:, None]) & (kc[None, :] < c0[:, None] + WIN_C)
    dcol = kc[None, :] - c[:, None] + WIN_C - 1
    pick = (dcol[:, :, None] == np.arange(2 * WIN_C - 1)[None, None, :]) & col_ok[:, :, None]
    toe = jnp.einsum('hrd,ckd->hrck', rel_bias.astype(F32), jnp.asarray(pick, F32),
                     precision=lax.Precision.HIGHEST)
    toe = jnp.where(jnp.asarray(col_ok)[None, None], toe, NEG_BIG)
    pad = NA_KROWS
    toe = jnp.pad(toe, ((0, 0), (pad, pad), (0, 0), (0, 0)), constant_values=NEG_BIG)
    i = np.arange(NA_KROWS)
    variants = []
    for var in range(3):
        per_row = []
        for rho in range(NA_QROWS):
            off = {0: rho, 1: rho + WIN_R // 2, 2: rho + WIN_R}[var]
            lo = {0: 0, 1: rho, 2: NA_QROWS}[var]
            row_ok = (i >= lo) & (i < lo + WIN_R)
            dr0 = -off + WIN_R - 1 + pad
            t = toe[:, dr0:dr0 + NA_KROWS]
            t = jnp.where(jnp.asarray(row_ok)[None, :, None, None], t, NEG_BIG)
            per_row.append(t.transpose(0, 2, 1, 3).reshape(NA_HEADS, GRID_W, NA_KROWS * GRID_W))
        variants.append(jnp.concatenate(per_row, axis=1))
    return jnp.stack(variants)


def _hgrn_masks(c):
    t_mat = lax.broadcasted_iota(I32, (c, c), 0)
    s_mat = lax.broadcasted_iota(I32, (c, c), 1)
    hb = c // 2
    x = lax.broadcasted_iota(I32, (hb, hb), 0) ^ lax.broadcasted_iota(I32, (hb, hb), 1)
    return (s_mat <= t_mat).astype(BF16), (s_mat >= t_mat).astype(BF16), x


def _hgrn_chunk(hq, hf, v, lb, st_ref, rev, tri, x):
    c = hq.shape[0]
    hq = hq.astype(F32)
    q = hq * _sigmoid(hq) * (HG_DK ** -0.5)
    s_pos, s_neg = _sigmoid_pair(hf.astype(F32))
    f = lb + (1.0 - lb) * s_pos
    logf = jnp.log(jnp.maximum(f, F_MIN))
    k = (1.0 - lb) * s_neg

    hb = c // 2
    t_col = lax.broadcasted_iota(I32, (c, 1), 0)
    hi, lo = _split2(logf)
    cs = _dot(tri, jnp.concatenate([hi, lo], axis=1))
    a = (cs[:, :HG_DK] + cs[:, HG_DK:]) * LOG2E

    st = st_ref[...]
    o_state = _dot_nt((q * jnp.exp2(a)).astype(BF16), st.astype(BF16))

    tl = lax.broadcasted_iota(I32, (hb, hb), 0)
    sl = lax.broadcasted_iota(I32, (hb, hb), 1)
    keep = (tl <= sl) if rev else (tl >= sl)
    dg = jnp.sum(q * k, axis=1, keepdims=True)
    att = [jnp.where(x == 0, dg[b * hb:(b + 1) * hb], 0.0) for b in range(2)]
    a8 = a.reshape(c // 8, 8, HG_DK)
    sub = lax.broadcasted_iota(I32, (1, 8, 1), 1)
    m = 1
    while m < hb:
        if m >= 4:
            ar = a.reshape(c // (2 * m), 2 * m, HG_DK)
            row = m if rev else m - 1
            ab = jnp.broadcast_to(ar[:, row:row + 1, :], ar.shape).reshape(c, HG_DK)
        else:
            off = m if rev else m - 1
            rows = [a8[:, p + off:p + off + 1, :] for p in range(0, 8, 2 * m)]
            ab = rows[-1]
            for idx in range(len(rows) - 2, -1, -1):
                ab = jnp.where(sub < (idx + 1) * 2 * m, rows[idx], ab)
            ab = jnp.broadcast_to(ab, a8.shape).reshape(c, HG_DK)
        e = jnp.exp2(_neg_abs(a - ab))
        upper = (t_col & m) != 0
        q_half = jnp.logical_not(upper) if rev else upper
        z = (jnp.where(q_half, q, k) * e).astype(BF16)
        for b in range(2):
            zb = z[b * hb:(b + 1) * hb]
            att[b] = jnp.where(x >= m, _dot_nt(zb, zb), att[b])
        m *= 2
    att = [jnp.where(keep, t, 0.0).astype(BF16) for t in att]

    qs, ks = (slice(0, hb), slice(hb, c)) if rev else (slice(hb, c), slice(0, hb))
    ab = a[hb:hb + 1] if rev else a[hb - 1:hb]
    qe = (q[qs] * jnp.exp2(_neg_abs(a[qs] - ab))).astype(BF16)
    ke = (k[ks] * jnp.exp2(_neg_abs(a[ks] - ab))).astype(BF16)
    cross = _dot(_dot_nt(qe, ke).astype(BF16), v[ks])
    o0 = _dot(att[0], v[:hb])
    o1 = _dot(att[1], v[hb:])
    o = o_state + (jnp.concatenate([o0 + cross, o1], axis=0) if rev else jnp.concatenate([o0, o1 + cross], axis=0))

    a_end = a[0:1, :] if rev else a[c - 1:c, :]
    k_end = (k * jnp.exp2(a_end - a)).astype(BF16)
    st_ref[...] = st * jnp.exp2(a_end) + _dot_tn(v, k_end)
    return o


def _mixers_body(q_ref, k_ref, v_ref, bias_ref, qf_ref, ff_ref, vf_ref, qb_ref, fb_ref, vb_ref, lb_ref,
                 na_ref, of_ref, ob_ref, st_ref):
    @pl.when(pl.program_id(1) == 0)
    def _():
        st_ref[...] = jnp.zeros_like(st_ref)

    q = q_ref[...]
    k = k_ref[0]
    v = v_ref[0]

    def na_head(h):
        sl = slice(h * NA_HD, (h + 1) * NA_HD)
        return _na_head(q[:, sl], k[:, sl], v[:, sl], bias_ref[h])

    tri_f, tri_b, x = _hgrn_masks(HG_CHUNK)
    na = []
    for h in range(HG_HEADS):
        sl = slice(h * HG_DK, (h + 1) * HG_DK)
        na.append(na_head(2 * h))
        of_ref[:, sl] = _hgrn_chunk(qf_ref[:, sl], ff_ref[:, sl], vf_ref[:, sl], lb_ref[0, h], st_ref.at[0, h],
                                    False, tri_f, x)
        na.append(na_head(2 * h + 1))
        ob_ref[:, sl] = _hgrn_chunk(qb_ref[:, sl], fb_ref[:, sl], vb_ref[:, sl], lb_ref[1, h], st_ref.at[1, h],
                                    True, tri_b, x)
    na_ref[...] = jnp.concatenate(na, axis=1).astype(BF16)


def _mixers(proj3, bias_tbl, lb4):
    b, l, _ = proj3.shape
    rows = l // GRID_W
    assert rows % NA_QROWS == 0 and rows >= 2 * WIN_R
    assert NA_QROWS * GRID_W == HG_CHUNK and NA_HEADS == 2 * HG_HEADS and NA_WIDTH == HG_HEADS * HG_DK
    n = rows // NA_QROWS
    blk = (None, HG_CHUNK, NA_WIDTH)

    def key_row0(g):
        return jnp.clip(NA_QROWS * g - WIN_R // 2, 0, rows - NA_KROWS)

    def variant(g):
        return jnp.where(g == 0, 0, jnp.where(g == n - 1, 2, 1))

    def spec(col, back=False):
        if back:
            return pl.BlockSpec(blk, lambda bi, g: (bi, n - 1 - g, col))
        return pl.BlockSpec(blk, lambda bi, g: (bi, g, col))

    kblk = (pl.Element(1), pl.Element(NA_KROWS * GRID_W), pl.Element(NA_WIDTH))
    return pl.pallas_call(
        _mixers_body,
        grid=(b, n),
        in_specs=[spec(COL_NA_Q),
                  pl.BlockSpec(kblk, lambda bi, g: (bi, key_row0(g) * GRID_W, COL_NA_K * NA_WIDTH)),
                  pl.BlockSpec(kblk, lambda bi, g: (bi, key_row0(g) * GRID_W, COL_NA_V * NA_WIDTH)),
                  pl.BlockSpec((None, NA_HEADS, NA_QROWS * GRID_W, NA_KROWS * GRID_W),
                               lambda bi, g: (variant(g), 0, 0, 0)),
                  spec(COL_HG_Q), spec(COL_HG_FF), spec(COL_HG_V),
                  spec(COL_HG_Q, True), spec(COL_HG_FB, True), spec(COL_HG_V, True),
                  pl.BlockSpec((2, HG_HEADS, 1, HG_DK), lambda bi, g: (0, 0, 0, 0))],
        out_specs=[spec(0), spec(0), spec(0, True)],
        out_shape=[jax.ShapeDtypeStruct((b, l, NA_WIDTH), BF16)] + [jax.ShapeDtypeStruct((b, l, NA_WIDTH), F32)] * 2,
        scratch_shapes=[pltpu.VMEM((2, HG_HEADS, HG_DV, HG_DK), F32)],
        compiler_params=_cparams(("parallel", "arbitrary"), 48),
        name="mixers_na_hgrn2",
    )(*([proj3] * 3), bias_tbl, *([proj3] * 6), lb4)


CONV_HALO = 16


def _conv_body(a_ref, b_ref, ap_ref, bp_ref, an_ref, bn_ref, w_ref, cb_ref, g_ref, be_ref, o_ref, u_ref, sh_ref):
    i = pl.program_id(1)
    n = pl.num_programs(1)
    t = a_ref.shape[0]

    def glu(a, b):
        return a[...].astype(F32) * _sigmoid(b[...].astype(F32))

    u_ref[pl.ds(0, CONV_HALO), :] = jnp.where(i > 0, glu(ap_ref, bp_ref), 0.0)
    u_ref[pl.ds(CONV_HALO, t), :] = glu(a_ref, b_ref)
    u_ref[pl.ds(CONV_HALO + t, CONV_HALO), :] = jnp.where(i < n - 1, glu(an_ref, bn_ref), 0.0)
    span = t + 2 * CONV_HALO - SUBLANES
    for r in range(1, SUBLANES):
        sh_ref[r - 1] = u_ref[pl.ds(r, span), :]
    base = CONV_HALO - CONV_K // 2
    acc = jnp.zeros((t, D_CONV), F32)
    for kk in range(CONV_K):
        off = base + kk
        r = off % SUBLANES
        win = u_ref[pl.ds(off, t), :] if r == 0 else sh_ref[r - 1, pl.ds(off - r, t), :]
        acc = acc + w_ref[kk:kk + 1, :] * win
    y = acc + cb_ref[...]
    mu = jnp.mean(y, axis=-1, keepdims=True)
    yc = y - mu
    var = jnp.mean(yc * yc, axis=-1, keepdims=True)
    yn = yc * lax.rsqrt(var + EPS) * g_ref[...] + be_ref[...]
    o_ref[...] = (yn * _sigmoid(yn)).astype(BF16)


def _conv(proj3, conv_w, conv_b, ln_g, ln_b, t=512):
    b, l, _ = proj3.shape
    nt = l // t
    hb = t // CONV_HALO
    nh = l // CONV_HALO
    main = lambda col: pl.BlockSpec((None, t, D_CONV), lambda bi, i: (bi, i, col))
    prev = lambda col: pl.BlockSpec((None, CONV_HALO, D_CONV), lambda bi, i: (bi, jnp.maximum(i * hb - 1, 0), col))
    nxt = lambda col: pl.BlockSpec((None, CONV_HALO, D_CONV), lambda bi, i: (bi, jnp.minimum((i + 1) * hb, nh - 1), col))
    vec = pl.BlockSpec((1, D_CONV), lambda bi, i: (0, 0))
    return pl.pallas_call(
        _conv_body,
        grid=(b, nt),
        in_specs=[main(COL_CV_A), main(COL_CV_B), prev(COL_CV_A), prev(COL_CV_B), nxt(COL_CV_A), nxt(COL_CV_B),
                  pl.BlockSpec((CONV_K, D_CONV), lambda bi, i: (0, 0)), vec, vec, vec],
        out_specs=pl.BlockSpec((None, t, D_CONV), lambda bi, i: (bi, i, 0)),
        out_shape=jax.ShapeDtypeStruct((b, l, D_CONV), BF16),
        scratch_shapes=[pltpu.VMEM((t + 2 * CONV_HALO, D_CONV), F32),
                        pltpu.VMEM((SUBLANES - 1, t + 2 * CONV_HALO - SUBLANES, D_CONV), F32)],
        compiler_params=_cparams(("parallel", "arbitrary"), 32),
        name="conv_module",
    )(*([proj3] * 6), conv_w, conv_b, ln_g, ln_b)


def _merge_body(na_ref, of_ref, ob_ref, hg_ref, cv_ref, gna_ref, ghg_ref, gcv_ref, x_ref,
                og_ref, wna_ref, whg_ref, wcv_ref, wout_ref, o_ref):
    o = of_ref[...] + ob_ref[...]
    parts = []
    for h in range(HG_HEADS):
        oh = o[:, h * HG_DV:(h + 1) * HG_DV]
        parts.append(oh * lax.rsqrt(jnp.mean(oh * oh, axis=-1, keepdims=True) + EPS))
    gg = hg_ref[...].astype(F32)
    hg = jnp.concatenate(parts, axis=1) * og_ref[...] * (gg * _sigmoid(gg))
    merged = (_sigmoid(gna_ref[...].astype(F32)) * _dot(na_ref[...], wna_ref[...])
              + _sigmoid(ghg_ref[...].astype(F32)) * _dot(hg.astype(BF16), whg_ref[...])
              + _sigmoid(gcv_ref[...].astype(F32)) * _dot(cv_ref[...], wcv_ref[...]))
    o_ref[...] = x_ref[...] + _dot(merged.astype(BF16), wout_ref[...])


def _merge(na2, of2, ob2, proj2, cv2, x2, og, wna, whg, wcv, wout, t=512):
    n = x2.shape[0]
    half = lambda col=0: pl.BlockSpec((t, 512), lambda i: (i, col))
    full = lambda col=0: pl.BlockSpec((t, D_MODEL), lambda i: (i, col))
    wspec = lambda r: pl.BlockSpec((r, D_MODEL), lambda i: (0, 0))
    return pl.pallas_call(
        _merge_body,
        grid=(n // t,),
        in_specs=[half(), half(), half(), half(COL_HG_G), half(), full(COL_G_NA), full(COL_G_HG), full(COL_G_CV),
                  full(), pl.BlockSpec((1, 512), lambda i: (0, 0)), wspec(512), wspec(512), wspec(512),
                  wspec(D_MODEL)],
        out_specs=full(),
        out_shape=jax.ShapeDtypeStruct((n, D_MODEL), F32),
        compiler_params=_cparams(("parallel",), 48),
        name="merge_out_proj",
    )(na2, of2, ob2, proj2, cv2, proj2, proj2, proj2, x2, og, wna, whg, wcv, wout)


def _memkv_body(m_ref, g_ref, w_ref, kn_ref, o_ref):
    kv = _dot(_rms_rows(m_ref[...], g_ref[...]).astype(BF16), w_ref[...])
    parts = []
    for h in range(MEM_HEADS):
        kh = kv[:, h * MEM_HD:(h + 1) * MEM_HD]
        parts.append(_rms_rows(kh, kn_ref[...]))
    parts.append(kv[:, MEM_WIDTH:])
    o_ref[...] = jnp.concatenate(parts, axis=1).astype(BF16)


def _memkv(mem2, g, wkv, kn, t=256):
    n = mem2.shape[0]
    return pl.pallas_call(
        _memkv_body,
        grid=(n // t,),
        in_specs=[pl.BlockSpec((t, D_MODEL), lambda i: (i, 0)), pl.BlockSpec((1, D_MODEL), lambda i: (0, 0)),
                  pl.BlockSpec((D_MODEL, 2 * MEM_WIDTH), lambda i: (0, 0)),
                  pl.BlockSpec((1, MEM_HD), lambda i: (0, 0))],
        out_specs=pl.BlockSpec((t, 2 * MEM_WIDTH), lambda i: (i, 0)),
        out_shape=jax.ShapeDtypeStruct((n, 2 * MEM_WIDTH), BF16),
        compiler_params=_cparams(("parallel",), 32),
        name="mem_kv",
    )(mem2, g, wkv, kn)


def _memattn_body(x_ref, g_ref, wq_ref, qn_ref, kv_ref, wo_ref, o_ref):
    x = x_ref[...]
    q = _dot(_rms_rows(x, g_ref[...]).astype(BF16), wq_ref[...])
    kv = kv_ref[...]
    outs = []
    for h in range(MEM_HEADS):
        sl = slice(h * MEM_HD, (h + 1) * MEM_HD)
        qh = _rms_rows(q[:, sl], qn_ref[...]) * (MEM_HD ** -0.5)
        s = _dot_nt(qh.astype(BF16), kv[:, sl])
        m = jnp.max(s, axis=-1, keepdims=True)
        p = jnp.exp(s - m)
        l = jnp.sum(p, axis=-1, keepdims=True)
        outs.append(_dot(p.astype(BF16), kv[:, MEM_WIDTH + h * MEM_HD:MEM_WIDTH + (h + 1) * MEM_HD]) / l)
    o_ref[...] = x + _dot(jnp.concatenate(outs, axis=1).astype(BF16), wo_ref[...])


def _memattn(x3, g, wq, qn, kv3, wo, t=512):
    b, l, _ = x3.shape
    m = kv3.shape[1]
    tok = pl.BlockSpec((None, t, D_MODEL), lambda bi, i: (bi, i, 0))
    const = lambda r, c: pl.BlockSpec((r, c), lambda bi, i: (0, 0))
    return pl.pallas_call(
        _memattn_body,
        grid=(b, l // t),
        in_specs=[tok, const(1, D_MODEL), const(D_MODEL, MEM_WIDTH), const(1, MEM_HD),
                  pl.BlockSpec((None, m, 2 * MEM_WIDTH), lambda bi, i: (bi, 0, 0)), const(MEM_WIDTH, D_MODEL)],
        out_specs=tok,
        out_shape=jax.ShapeDtypeStruct((b, l, D_MODEL), F32),
        compiler_params=_cparams(("parallel", "arbitrary"), 32),
        name="mem_attention",
    )(x3, g, wq, qn, kv3, wo)


def _router_body(x_ref, g_ref, w_ref, h_ref, a_ref):
    hn = _rms_rows(x_ref[...], g_ref[...])
    h_ref[...] = hn.astype(BF16)
    h_hi, h_lo = _split2(hn)
    w_hi, w_lo = _split2(w_ref[...])
    both = _dot(h_hi, jnp.concatenate([w_hi, w_lo], axis=1))
    logits = both[:, :N_EXPERTS] + both[:, N_EXPERTS:] + _dot(h_lo, w_hi)
    e = jnp.exp(logits - jnp.max(logits, axis=-1, keepdims=True))
    a_ref[...] = e / jnp.sum(e, axis=-1, keepdims=True)


def _router(x2, g, w_router, t=512):
    n = x2.shape[0]
    return pl.pallas_call(
        _router_body,
        grid=(n // t,),
        in_specs=[pl.BlockSpec((t, D_MODEL), lambda i: (i, 0)), pl.BlockSpec((1, D_MODEL), lambda i: (0, 0)),
                  pl.BlockSpec((D_MODEL, N_EXPERTS), lambda i: (0, 0))],
        out_specs=[pl.BlockSpec((t, D_MODEL), lambda i: (i, 0)), pl.BlockSpec((t, N_EXPERTS), lambda i: (i, 0))],
        out_shape=[jax.ShapeDtypeStruct((n, D_MODEL), BF16), jax.ShapeDtypeStruct((n, N_EXPERTS), F32)],
        compiler_params=_cparams(("parallel",), 32),
        name="router",
    )(x2, g, w_router)


def _thresh_body(a_ref, thr_ref, need_ref, *, cap):
    bits = pltpu.bitcast(a_ref[...], I32)

    def step(b, thr):
        cand = thr | (1 << (30 - b))
        cnt = jnp.sum((bits >= cand).astype(I32), axis=1, keepdims=True)
        return jnp.where(cnt >= cap, cand, thr)

    thr = lax.fori_loop(0, 31, step, jnp.zeros((N_EXPERTS, 1), I32))
    thr_ref[...] = thr
    need_ref[...] = cap - jnp.sum((bits > thr).astype(I32), axis=1, keepdims=True)


def _thresh(aff_t, cap):
    n = aff_t.shape[1]
    return pl.pallas_call(
        functools.partial(_thresh_body, cap=cap),
        grid=(1,),
        in_specs=[pl.BlockSpec((N_EXPERTS, n), lambda i: (0, 0))],
        out_specs=[pl.BlockSpec((N_EXPERTS, 1), lambda i: (0, 0))] * 2,
        out_shape=[jax.ShapeDtypeStruct((N_EXPERTS, 1), I32)] * 2,
        compiler_params=_cparams(("arbitrary",), 32),
        name="topk_threshold",
    )(aff_t)


def _slots_body(a_ref, thr_ref, need_ref, pos_ref, eq_off, sel_off):
    @pl.when(pl.program_id(0) == 0)
    def _():
        eq_off[...] = jnp.zeros_like(eq_off)
        sel_off[...] = jnp.zeros_like(sel_off)

    bits = pltpu.bitcast(a_ref[...], I32)
    w = bits.shape[1]
    thr = thr_ref[...]
    before = (lax.broadcasted_iota(I32, (w, w), 0) < lax.broadcasted_iota(I32, (w, w), 1)).astype(BF16)
    eq = bits == thr
    eq_f = eq.astype(F32)
    eq_rank = _dot(eq_f.astype(BF16), before) + eq_off[...]
    sel = (bits > thr) | (eq & (eq_rank < need_ref[...].astype(F32)))
    sel_f = sel.astype(F32)
    slot = _dot(sel_f.astype(BF16), before) + sel_off[...]
    pos_ref[...] = jnp.where(sel, slot, -1.0).astype(I32)
    eq_off[...] += jnp.sum(eq_f, axis=1, keepdims=True)
    sel_off[...] += jnp.sum(sel_f, axis=1, keepdims=True)


def _slots(aff_t, thr, need, w=TOK_BLK):
    n = aff_t.shape[1]
    col = pl.BlockSpec((N_EXPERTS, 1), lambda i: (0, 0))
    return pl.pallas_call(
        _slots_body,
        grid=(n // w,),
        in_specs=[pl.BlockSpec((N_EXPERTS, w), lambda i: (0, i)), col, col],
        out_specs=pl.BlockSpec((N_EXPERTS, w), lambda i: (0, i)),
        out_shape=jax.ShapeDtypeStruct((N_EXPERTS, n), I32),
        scratch_shapes=[pltpu.VMEM((N_EXPERTS, 1), F32)] * 2,
        compiler_params=_cparams(("arbitrary",), 32),
        name="topk_slots",
    )(aff_t, thr, need)


def _gather_body(cum_ref, nr_ref, h_ref, pos_ref, xe_hbm, stg, sem, *, cap):
    i = pl.program_id(0)
    nblk = pl.num_programs(0)
    win_rows = COMB_WIN * ROW_TILE

    def copies(blk, r, buf):
        return [pltpu.make_async_copy(
            stg.at[buf, pl.ds(e * win_rows, win_rows), :],
            xe_hbm.at[e, pl.ds(pl.multiple_of((cum_ref[e, blk] + r * COMB_WIN) * ROW_TILE, ROW_TILE), win_rows), :],
            sem.at[buf]) for e in range(N_EXPERTS)]

    w_iota = lax.broadcasted_iota(I32, (COMB_WIN, TOK_BLK), 0)

    def stage(r, buf):
        sel = []
        for e in range(N_EXPERTS):
            nominal = cum_ref[e, i] + r * COMB_WIN
            p = pos_ref[e:e + 1, :]
            sel.append((w_iota == jnp.where(p >= nominal, p - nominal, -1)).astype(BF16))
        res = _dot(jnp.concatenate(sel, axis=0), h_ref[...])
        for j in range(ROW_TILE):
            stg[buf, pl.ds(j, N_EXPERTS * COMB_WIN, stride=ROW_TILE), :] = res[:, j * LANES:(j + 1) * LANES]

    @pl.when(i == 0)
    def _():
        stg[2, pl.ds(0, win_rows), :] = jnp.zeros((win_rows, LANES), F32)
        pads = [pltpu.make_async_copy(stg.at[2, pl.ds(0, win_rows), :],
                                      xe_hbm.at[e, pl.ds(cap * ROW_TILE, win_rows), :], sem.at[2])
                for e in range(N_EXPERTS)]
        for c in pads:
            c.start()
        for c in pads:
            c.wait()

    slot = i % 2
    nr = nr_ref[i]
    stage(0, slot)

    @pl.when((i > 0) & (nr_ref[jnp.maximum(i - 1, 0)] == 1))
    def _():
        for c in copies(i - 1, 0, 1 - slot):
            c.wait()

    for c in copies(i, 0, slot):
        c.start()

    @pl.when(nr > 1)
    def _():
        for c in copies(i, 0, slot):
            c.wait()

        def extra(r, carry):
            stage(r, 2)
            for e, c in enumerate(copies(i, r, 2)):
                @pl.when(cum_ref[e, i + 1] > cum_ref[e, i] + r * COMB_WIN)
                def _():
                    c.start()
                    c.wait()
            return carry

        lax.fori_loop(1, nr, extra, 0)

    @pl.when((i == nblk - 1) & (nr == 1))
    def _():
        for c in copies(i, 0, slot):
            c.wait()


def _gather(cum, nrounds, hn, pos, cap):
    n = hn.shape[0]
    return pl.pallas_call(
        functools.partial(_gather_body, cap=cap),
        grid_spec=pltpu.PrefetchScalarGridSpec(
            num_scalar_prefetch=2,
            grid=(n // TOK_BLK,),
            in_specs=[pl.BlockSpec((TOK_BLK, D_MODEL), lambda i, cum, nr: (i, 0)),
                      pl.BlockSpec((N_EXPERTS, TOK_BLK), lambda i, cum, nr: (0, i))],
            out_specs=pl.BlockSpec(memory_space=pl.ANY),
            scratch_shapes=[pltpu.VMEM((3, N_EXPERTS * COMB_WIN * ROW_TILE, LANES), F32),
                            pltpu.SemaphoreType.DMA((3,))],
        ),
        out_shape=jax.ShapeDtypeStruct((N_EXPERTS, (cap + COMB_WIN) * ROW_TILE, LANES), F32),
        compiler_params=_cparams(("arbitrary",), 40),
        name="expert_gather",
    )(cum, nrounds, hn, pos)


def _ffn_body(x_ref, wg_ref, wu_ref, wd_ref, o_ref, acc_ref, xb_ref):
    f = pl.program_id(2)

    @pl.when(f == 0)
    def _():
        tm = xb_ref.shape[0]
        xb_ref[...] = jnp.concatenate([x_ref[pl.ds(j, tm, stride=ROW_TILE), :] for j in range(ROW_TILE)],
                                      axis=1).astype(BF16)

    x = xb_ref[...]
    gate = _dot(x, wg_ref[...])
    up = _dot(x, wu_ref[...])
    part = _dot((gate * _sigmoid(gate) * up).astype(BF16), wd_ref[...])

    last = pl.num_programs(2) - 1

    @pl.when(f == 0)
    def _():
        acc_ref[...] = part

    @pl.when((f != 0) & (f != last))
    def _():
        acc_ref[...] += part

    @pl.when(f == last)
    def _():
        total = acc_ref[...] + part
        tm = total.shape[0]
        for j in range(ROW_TILE):
            o_ref[pl.ds(j, tm, stride=ROW_TILE), :] = total[:, j * LANES:(j + 1) * LANES]


def _ffn(xe, wgu, wd, cap, tm=1024, tf=D_FF_PAD // 2):
    e = xe.shape[0]
    tm = min(tm, cap)
    nf = D_FF_PAD // tf
    assert nf >= 2
    return pl.pallas_call(
        _ffn_body,
        grid=(e, cap // tm, nf),
        in_specs=[pl.BlockSpec((None, tm * ROW_TILE, LANES), lambda ei, i, f: (ei, i, 0)),
                  pl.BlockSpec((None, D_MODEL, tf), lambda ei, i, f: (ei, 0, f)),
                  pl.BlockSpec((None, D_MODEL, tf), lambda ei, i, f: (ei, 0, f + nf)),
                  pl.BlockSpec((None, tf, D_MODEL), lambda ei, i, f: (ei, f, 0))],
        out_specs=pl.BlockSpec((None, tm * ROW_TILE, LANES), lambda ei, i, f: (ei, i, 0)),
        out_shape=jax.ShapeDtypeStruct((e, cap * ROW_TILE, LANES), F32),
        scratch_shapes=[pltpu.VMEM((tm, D_MODEL), F32), pltpu.VMEM((tm, D_MODEL), BF16)],
        compiler_params=_cparams(("parallel", "parallel", "arbitrary"), 56),
        name="expert_ffn",
    )(xe, wgu, wgu, wd)


def _combine_body(cum_ref, nr_ref, x_ref, aff_ref, pos_ref, y_hbm, o_ref, ybuf, sem, *, cap):
    i = pl.program_id(0)
    nblk = pl.num_programs(0)
    win_rows = COMB_WIN * ROW_TILE

    def win_start(blk, e, r):
        return jnp.minimum(cum_ref[e, blk] + r * COMB_WIN, cap - COMB_WIN)

    def copies(blk, r, buf):
        return [pltpu.make_async_copy(
            y_hbm.at[e, pl.ds(pl.multiple_of(win_start(blk, e, r) * ROW_TILE, ROW_TILE), win_rows), :],
            ybuf.at[buf, pl.ds(e * win_rows, win_rows), :],
            sem.at[buf]) for e in range(N_EXPERTS)]

    lane = lax.broadcasted_iota(I32, (TOK_BLK, LANES), 1)
    hi_half = lane >= COMB_WIN
    lane_w = jnp.where(hi_half, lane - COMB_WIN, lane)

    def contribution(r, buf):
        y = jnp.concatenate([ybuf[buf, pl.ds(j, N_EXPERTS * COMB_WIN, stride=ROW_TILE), :]
                             for j in range(ROW_TILE)], axis=1).astype(BF16)
        pos = pos_ref[...]
        aff = aff_ref[...]

        def col(e):
            nominal = cum_ref[e, i] + r * COMB_WIN
            p = pos[:, e:e + 1]
            return jnp.where(p >= nominal, p - win_start(i, e, r), -1)

        parts = []
        for e in range(0, N_EXPERTS, 2):
            d = jnp.where(hi_half, col(e + 1), col(e))
            g = jnp.where(hi_half, aff[:, e + 1:e + 2], aff[:, e:e + 1])
            parts.append(jnp.where(d == lane_w, g, 0.0).astype(BF16))
        return _dot(jnp.concatenate(parts, axis=1), y)

    slot = i % 2

    @pl.when(i == 0)
    def _():
        for c in copies(0, 0, 0):
            c.start()

    @pl.when(i + 1 < nblk)
    def _():
        for c in copies(i + 1, 0, 1 - slot):
            c.start()

    for c in copies(i, 0, slot):
        c.wait()
    o_ref[...] = x_ref[...] + contribution(0, slot)

    def extra(r, carry):
        for c in copies(i, r, 2):
            c.start()
        for c in copies(i, r, 2):
            c.wait()
        o_ref[...] += contribution(r, 2)
        return carry

    lax.fori_loop(1, nr_ref[i], extra, 0)


def _combine(cum, nrounds, x2, aff, pos_t, ye, cap):
    n = x2.shape[0]
    assert 2 * COMB_WIN == LANES and cap >= COMB_WIN
    tok = lambda w: pl.BlockSpec((TOK_BLK, w), lambda i, cum, nr: (i, 0))
    return pl.pallas_call(
        functools.partial(_combine_body, cap=cap),
        grid_spec=pltpu.PrefetchScalarGridSpec(
            num_scalar_prefetch=2,
            grid=(n // TOK_BLK,),
            in_specs=[tok(D_MODEL), tok(N_EXPERTS), tok(N_EXPERTS), pl.BlockSpec(memory_space=pl.ANY)],
            out_specs=tok(D_MODEL),
            scratch_shapes=[pltpu.VMEM((3, N_EXPERTS * COMB_WIN * ROW_TILE, LANES), F32),
                            pltpu.SemaphoreType.DMA((3,))],
        ),
        out_shape=jax.ShapeDtypeStruct((n, D_MODEL), F32),
        compiler_params=_cparams(("arbitrary",), 40),
        name="expert_combine",
    )(cum, nrounds, x2, aff, pos_t, ye)


def _expert_choice(x2, hn, aff, wgu, wd):
    n = x2.shape[0]
    cap = EC_CAPACITY * n // N_EXPERTS
    aff_t = aff.T
    thr, need = _thresh(aff_t, cap)
    pos = _slots(aff_t, thr, need)
    cnt = jnp.sum((pos >= 0).reshape(N_EXPERTS, n // TOK_BLK, TOK_BLK), axis=-1, dtype=I32)
    cum = jnp.concatenate([jnp.zeros((N_EXPERTS, 1), I32), jnp.cumsum(cnt, axis=1, dtype=I32)], axis=1)
    nrounds = jnp.maximum(1, (jnp.max(cnt, axis=0) + COMB_WIN - 1) // COMB_WIN).astype(I32)
    xe = _gather(cum, nrounds, hn, pos, cap)
    ye = _ffn(xe, wgu, wd, cap)
    return _combine(cum, nrounds, x2, aff, pos.T, ye, cap)


def _pad_gate_up_body(x_ref, o_ref):
    x = x_ref[...]
    zeros = jnp.zeros((x.shape[0], D_FF_PAD - D_FF), BF16)
    o_ref[:, 0:D_FF] = x[:, :D_FF].astype(BF16)
    o_ref[:, D_FF:D_FF_PAD] = zeros
    o_ref[:, D_FF_PAD:D_FF_PAD + D_FF] = x[:, D_FF:].astype(BF16)
    o_ref[:, D_FF_PAD + D_FF:] = zeros


def _pad_gate_up(w, layer, t=256):
    e = w.shape[1]
    return pl.pallas_call(
        _pad_gate_up_body,
        grid=(e, D_MODEL // t),
        in_specs=[pl.BlockSpec((None, None, t, 2 * D_FF), lambda ei, i: (layer, ei, i, 0))],
        out_specs=pl.BlockSpec((None, t, 2 * D_FF_PAD), lambda ei, i: (ei, i, 0)),
        out_shape=jax.ShapeDtypeStruct((e, D_MODEL, 2 * D_FF_PAD), BF16),
        compiler_params=_cparams(("parallel", "parallel"), 32),
        name="pad_gate_up",
    )(w)


def _pad_down_body(x_ref, o_ref):
    x = x_ref[...]
    row = lax.broadcasted_iota(I32, x.shape, 0) + pl.program_id(1) * x.shape[0]
    o_ref[...] = jnp.where(row < D_FF, x, 0.0).astype(BF16)


def _pad_down(w, layer, t=D_FF_PAD // 4):
    e = w.shape[1]
    return pl.pallas_call(
        _pad_down_body,
        grid=(e, D_FF_PAD // t),
        in_specs=[pl.BlockSpec((None, None, t, D_MODEL), lambda ei, i: (layer, ei, i, 0))],
        out_specs=pl.BlockSpec((None, t, D_MODEL), lambda ei, i: (ei, i, 0)),
        out_shape=jax.ShapeDtypeStruct((e, D_FF_PAD, D_MODEL), BF16),
        compiler_params=_cparams(("parallel", "parallel"), 32),
        name="pad_down",
    )(w)


def _prep_layer(l, p, lb):
    row = lambda a: a.astype(F32).reshape(1, -1)
    qkg = jnp.concatenate([jnp.tile(p["na_q_norm"][l] * (NA_HD ** -0.5), NA_HEADS),
                           jnp.tile(p["na_k_norm"][l], NA_HEADS)]).reshape(1, 2 * NA_WIDTH)
    idx = jnp.arange(MXU_DIM) // NA_HD
    return dict(
        norm_mix=row(p["norm_mix"][l]), w_in=p["w_in"][l].astype(BF16),
        grp=(idx[:, None] == idx[None, :]).astype(BF16), qkg=qkg.astype(F32),
        bias_tbl=_na_bias_table(p["na_rel_bias"][l]),
        lb4=lb[:, l].reshape(2, HG_HEADS, 1, HG_DK),
        conv_w=p["conv_w"][l].astype(F32), conv_b=row(p["conv_b"][l]),
        ln_g=row(p["conv_ln_g"][l]), ln_b=row(p["conv_ln_b"][l]),
        og=jnp.tile(p["hg_out_norm"][l], HG_HEADS).reshape(1, -1).astype(F32),
        wna=p["w_na_br"][l].astype(BF16), whg=p["w_hg_br"][l].astype(BF16), wcv=p["w_cv_br"][l].astype(BF16),
        wout=p["w_out"][l].astype(BF16),
        norm_mem=row(p["norm_mem"][l]), mem_kv_norm=row(p["mem_kv_norm"][l]),
        wq=p["wq_mem"][l].astype(BF16), wkv=p["wkv_mem"][l].astype(BF16),
        qn=row(p["mem_q_norm"][l]), kn=row(p["mem_k_norm"][l]), wo=p["wo_mem"][l].astype(BF16),
        norm_ffn=row(p["norm_ffn"][l]), w_router=p["w_router"][l].astype(F32),
        wgu=_pad_gate_up(p["w_gate_up"], l), wd=_pad_down(p["w_down"], l),
    )


def _layer(x3, mem3, w):
    b, l, _ = x3.shape
    n = b * l
    x2 = x3.reshape(n, D_MODEL)
    proj2 = _inproj(x2, w["norm_mix"], w["w_in"], w["grp"], w["qkg"])
    proj3 = proj2.reshape(b, l, D_IN)
    na, o_f, o_b = _mixers(proj3, w["bias_tbl"], w["lb4"])
    cv = _conv(proj3, w["conv_w"], w["conv_b"], w["ln_g"], w["ln_b"])
    x2 = _merge(na.reshape(n, -1), o_f.reshape(n, -1), o_b.reshape(n, -1), proj2, cv.reshape(n, -1), x2,
                w["og"], w["wna"], w["whg"], w["wcv"], w["wout"])
    kv = _memkv(mem3.reshape(-1, D_MODEL), w["mem_kv_norm"], w["wkv"], w["kn"])
    x3 = _memattn(x2.reshape(b, l, D_MODEL), w["norm_mem"], w["wq"], w["qn"],
                  kv.reshape(b, -1, 2 * MEM_WIDTH), w["wo"])
    x2 = x3.reshape(n, D_MODEL)
    hn, aff = _router(x2, w["norm_ffn"], w["w_router"])
    x2 = _expert_choice(x2, hn, aff, w["wgu"], w["wd"])
    return x2.reshape(b, l, D_MODEL)


def kernel(x_prompt, x_sample, mem_prompt, mem_sample, norm_mix, w_in, na_q_norm, na_k_norm, na_rel_bias, w_na_br, hg_lb, hg_out_norm, w_hg_br, conv_w, conv_b, conv_ln_g, conv_ln_b, w_cv_br, w_out, norm_mem, mem_kv_norm, wq_mem, wkv_mem, mem_q_norm, mem_k_norm, wo_mem, norm_ffn, w_router, w_gate_up, w_down):
    p = dict(norm_mix=norm_mix, w_in=w_in, na_q_norm=na_q_norm, na_k_norm=na_k_norm, na_rel_bias=na_rel_bias,
             w_na_br=w_na_br, hg_out_norm=hg_out_norm, w_hg_br=w_hg_br, conv_w=conv_w, conv_b=conv_b,
             conv_ln_g=conv_ln_g, conv_ln_b=conv_ln_b, w_cv_br=w_cv_br, w_out=w_out, norm_mem=norm_mem,
             mem_kv_norm=mem_kv_norm, wq_mem=wq_mem, wkv_mem=wkv_mem, mem_q_norm=mem_q_norm,
             mem_k_norm=mem_k_norm, wo_mem=wo_mem, norm_ffn=norm_ffn, w_router=w_router, w_gate_up=w_gate_up,
             w_down=w_down)
    depth = w_in.shape[0]
    sm = jax.nn.softmax(hg_lb.astype(F32), axis=1)
    lb = jnp.cumsum(sm, axis=1) - sm[:, :1]
    xs = [x_prompt, x_sample]
    mems = [mem_prompt, mem_sample]
    for l in range(depth):
        w = _prep_layer(l, p, lb)
        xs = [_layer(x, m, w) for x, m in zip(xs, mems)]
    return tuple(xs)
```

```python
import functools

import jax
import jax.numpy as jnp
import numpy as np
from jax import lax
from jax.experimental import pallas as pl
from jax.experimental.pallas import tpu as pltpu

F32 = jnp.float32
BF16 = jnp.bfloat16
I32 = jnp.int32

D_MODEL = 1024
GRID_W = 64
NA_HEADS = 8
NA_HD = 64
NA_WIDTH = NA_HEADS * NA_HD
WIN_R = 8
WIN_C = 16
HG_HEADS = 4
HG_DK = 128
HG_DV = 128
HG_KW = HG_HEADS * HG_DK
F_MIN = 1e-20
D_CONV = 512
CONV_K = 31
MEM_HEADS = 4
MEM_HD = 128
MEM_WIDTH = MEM_HEADS * MEM_HD
N_EXPERTS = 16
EC_CAPACITY = 2
D_FF = 2752
EPS = 1e-6
D_IN = 8192

COL_NA_Q, COL_NA_K, COL_NA_V = 0, 1, 2
COL_HG_Q, COL_HG_FF, COL_HG_FB, COL_HG_V, COL_HG_G = 3, 4, 5, 6, 7
COL_CV_A, COL_CV_B = 8, 9
COL_G_NA, COL_G_HG, COL_G_CV = 5, 6, 7

LANES = 128
SUBLANES = 8
MXU_DIM = 256
D_FF_PAD = 2816
NEG_BIG = -1e30
LOG2E = 1.4426950408889634

HG_CHUNK = 256
TOK_BLK = 256
ROW_TILE = D_MODEL // LANES
COMB_WIN = 64


def _cparams(sem, vmem_mb):
    return pltpu.CompilerParams(dimension_semantics=sem, vmem_limit_bytes=vmem_mb << 20)


def _sigmoid_pair(x):
    t = 0.5 * jnp.tanh(0.5 * x)
    return 0.5 + t, 0.5 - t


def _sigmoid(x):
    return 0.5 + 0.5 * jnp.tanh(0.5 * x)


def _neg_abs(x):
    return lax.bitcast_convert_type(lax.bitcast_convert_type(x, I32) | jnp.int32(-2 ** 31), F32)


def _split2(a):
    hi = a.astype(BF16)
    lo = (a - hi.astype(F32)).astype(BF16)
    return hi, lo


def _dot(a, b):
    return jnp.dot(a, b, preferred_element_type=F32)


def _dot_nt(a, b):
    return lax.dot_general(a, b, (((1,), (1,)), ((), ())), preferred_element_type=F32)


def _dot_tn(a, b):
    return lax.dot_general(a, b, (((0,), (0,)), ((), ())), preferred_element_type=F32)


def _rms_rows(x, g):
    return x * lax.rsqrt(jnp.mean(x * x, axis=-1, keepdims=True) + EPS) * g


def _inproj_body(x_ref, g_ref, w_ref, grp_ref, qkg_ref, o_ref, h_ref):
    j = pl.program_id(1)

    @pl.when(j == 0)
    def _():
        h_ref[...] = _rms_rows(x_ref[...], g_ref[...]).astype(BF16)

    acc = _dot(h_ref[...], w_ref[...])

    @pl.when(j == 0)
    def _():
        nqk = qkg_ref.shape[1]
        qk = acc[:, :nqk]
        sq = (qk * qk).astype(BF16)
        w = grp_ref.shape[0]
        ss = jnp.concatenate([_dot(sq[:, c * w:(c + 1) * w], grp_ref[...]) for c in range(nqk // w)], axis=1)
        o_ref[:, :nqk] = (qk * lax.rsqrt(ss * (1.0 / NA_HD) + EPS) * qkg_ref[...]).astype(BF16)
        o_ref[:, nqk:] = acc[:, nqk:].astype(BF16)

    @pl.when(j != 0)
    def _():
        o_ref[...] = acc.astype(BF16)


def _inproj(x2, g, w_bf, grp, qkg, tm=1024, tn=2048):
    n = x2.shape[0]
    assert tn > qkg.shape[1]
    return pl.pallas_call(
        _inproj_body,
        grid=(n // tm, D_IN // tn),
        in_specs=[
            pl.BlockSpec((tm, D_MODEL), lambda i, j: (i, 0)),
            pl.BlockSpec((1, D_MODEL), lambda i, j: (0, 0)),
            pl.BlockSpec((D_MODEL, tn), lambda i, j: (0, j)),
            pl.BlockSpec((MXU_DIM, MXU_DIM), lambda i, j: (0, 0)),
            pl.BlockSpec((1, 2 * NA_WIDTH), lambda i, j: (0, 0)),
        ],
        out_specs=pl.BlockSpec((tm, tn), lambda i, j: (i, j)),
        out_shape=jax.ShapeDtypeStruct((n, D_IN), BF16),
        scratch_shapes=[pltpu.VMEM((tm, D_MODEL), BF16)],
        compiler_params=_cparams(("parallel", "arbitrary"), 48),
        name="in_proj",
    )(x2, g, w_bf, grp, qkg)


NA_QROWS = 4
NA_KROWS = NA_QROWS + WIN_R


def _na_head(q, k, v, bias):
    s = _dot_nt(q, k) + bias
    p = jnp.exp(s - jnp.max(s, axis=-1, keepdims=True))
    return _dot(p.astype(BF16), v) / jnp.sum(p, axis=-1, keepdims=True)


def _na_bias_table(rel_bias):
    c = np.arange(GRID_W)
    c0 = np.clip(c - WIN_C // 2, 0, GRID_W - WIN_C)
    kc = np.arange(GRID_W)
    col_ok = (kc[None, :] >= c0[:, None]) & (kc[None, :] < c0[:, None] + WIN_C)
    dcol = kc[None, :] - c[:, None] + WIN_C - 1
    pick = (dcol[:, :, None] == np.arange(2 * WIN_C - 1)[None, None, :]) & col_ok[:, :, None]
    toe = jnp.einsum('hrd,ckd->hrck', rel_bias.astype(F32), jnp.asarray(pick, F32),
                     precision=lax.Precision.HIGHEST)
    toe = jnp.where(jnp.asarray(col_ok)[None, None], toe, NEG_BIG)
    pad = NA_KROWS
    toe = jnp.pad(toe, ((0, 0), (pad, pad), (0, 0), (0, 0)), constant_values=NEG_BIG)
    i = np.arange(NA_KROWS)
    variants = []
    for var in range(3):
        per_row = []
        for rho in range(NA_QROWS):
            off = {0: rho, 1: rho + WIN_R // 2, 2: rho + WIN_R}[var]
            lo = {0: 0, 1: rho, 2: NA_QROWS}[var]
            row_ok = (i >= lo) & (i < lo + WIN_R)
            dr0 = -off + WIN_R - 1 + pad
            t = toe[:, dr0:dr0 + NA_KROWS]
            t = jnp.where(jnp.asarray(row_ok)[None, :, None, None], t, NEG_BIG)
            per_row.append(t.transpose(0, 2, 1, 3).reshape(NA_HEADS, GRID_W, NA_KROWS * GRID_W))
        variants.append(jnp.concatenate(per_row, axis=1))
    return jnp.stack(variants)


def _hgrn_masks(c):
    t_mat = lax.broadcasted_iota(I32, (c, c), 0)
    s_mat = lax.broadcasted_iota(I32, (c, c), 1)
    hb = c // 2
    x = lax.broadcasted_iota(I32, (hb, hb), 0) ^ lax.broadcasted_iota(I32, (hb, hb), 1)
    return (s_mat <= t_mat).astype(BF16), (s_mat >= t_mat).astype(BF16), x


def _hgrn_chunk(hq, hf, v, lb, st_ref, rev, tri, x):
    c = hq.shape[0]
    hq = hq.astype(F32)
    q = hq * _sigmoid(hq) * (HG_DK ** -0.5)
    s_pos, s_neg = _sigmoid_pair(hf.astype(F32))
    f = lb + (1.0 - lb) * s_pos
    logf = jnp.log(jnp.maximum(f, F_MIN))
    k = (1.0 - lb) * s_neg

    hb = c // 2
    t_col = lax.broadcasted_iota(I32, (c, 1), 0)
    hi, lo = _split2(logf)
    cs = _dot(tri, jnp.concatenate([hi, lo], axis=1))
    a = (cs[:, :HG_DK] + cs[:, HG_DK:]) * LOG2E

    st = st_ref[...]
    o_state = _dot_nt((q * jnp.exp2(a)).astype(BF16), st.astype(BF16))

    tl = lax.broadcasted_iota(I32, (hb, hb), 0)
    sl = lax.broadcasted_iota(I32, (hb, hb), 1)
    keep = (tl <= sl) if rev else (tl >= sl)
    dg = jnp.sum(q * k, axis=1, keepdims=True)
    att = [jnp.where(x == 0, dg[b * hb:(b + 1) * hb], 0.0) for b in range(2)]
    a8 = a.reshape(c // 8, 8, HG_DK)
    sub = lax.broadcasted_iota(I32, (1, 8, 1), 1)
    m = 1
    while m < hb:
        if m >= 4:
            ar = a.reshape(c // (2 * m), 2 * m, HG_DK)
            row = m if rev else m - 1
            ab = jnp.broadcast_to(ar[:, row:row + 1, :], ar.shape).reshape(c, HG_DK)
        else:
            off = m if rev else m - 1
            rows = [a8[:, p + off:p + off + 1, :] for p in range(0, 8, 2 * m)]
            ab = rows[-1]
            for idx in range(len(rows) - 2, -1, -1):
                ab = jnp.where(sub < (idx + 1) * 2 * m, rows[idx], ab)
            ab = jnp.broadcast_to(ab, a8.shape).reshape(c, HG_DK)
        e = jnp.exp2(_neg_abs(a - ab))
        upper = (t_col & m) != 0
        q_half = jnp.logical_not(upper) if rev else upper
        z = (jnp.where(q_half, q, k) * e).astype(BF16)
        for b in range(2):
            zb = z[b * hb:(b + 1) * hb]
            att[b] = jnp.where(x >= m, _dot_nt(zb, zb), att[b])
        m *= 2
    att = [jnp.where(keep, t, 0.0).astype(BF16) for t in att]

    qs, ks = (slice(0, hb), slice(hb, c)) if rev else (slice(hb, c), slice(0, hb))
    ab = a[hb:hb + 1] if rev else a[hb - 1:hb]
    qe = (q[qs] * jnp.exp2(_neg_abs(a[qs] - ab))).astype(BF16)
    ke = (k[ks] * jnp.exp2(_neg_abs(a[ks] - ab))).astype(BF16)
    cross = _dot(_dot_nt(qe, ke).astype(BF16), v[ks])
    o0 = _dot(att[0], v[:hb])
    o1 = _dot(att[1], v[hb:])
    o = o_state + (jnp.concatenate([o0 + cross, o1], axis=0) if rev else jnp.concatenate([o0, o1 + cross], axis=0))

    a_end = a[0:1, :] if rev else a[c - 1:c, :]
    k_end = (k * jnp.exp2(a_end - a)).astype(BF16)
    st_ref[...] = st * jnp.exp2(a_end) + _dot_tn(v, k_end)
    return o


def _mixers_body(q_ref, k_ref, v_ref, bias_ref, qf_ref, ff_ref, vf_ref, qb_ref, fb_ref, vb_ref, lb_ref,
                 na_ref, of_ref, ob_ref, st_ref):
    @pl.when(pl.program_id(1) == 0)
    def _():
        st_ref[...] = jnp.zeros_like(st_ref)

    q = q_ref[...]
    k = k_ref[0]
    v = v_ref[0]

    def na_head(h):
        sl = slice(h * NA_HD, (h + 1) * NA_HD)
        return _na_head(q[:, sl], k[:, sl], v[:, sl], bias_ref[h])

    tri_f, tri_b, x = _hgrn_masks(HG_CHUNK)
    na = []
    for h in range(HG_HEADS):
        sl = slice(h * HG_DK, (h + 1) * HG_DK)
        na.append(na_head(2 * h))
        of_ref[:, sl] = _hgrn_chunk(qf_ref[:, sl], ff_ref[:, sl], vf_ref[:, sl], lb_ref[0, h], st_ref.at[0, h],
                                    False, tri_f, x)
        na.append(na_head(2 * h + 1))
        ob_ref[:, sl] = _hgrn_chunk(qb_ref[:, sl], fb_ref[:, sl], vb_ref[:, sl], lb_ref[1, h], st_ref.at[1, h],
                                    True, tri_b, x)
    na_ref[...] = jnp.concatenate(na, axis=1).astype(BF16)


def _mixers(proj3, bias_tbl, lb4):
    b, l, _ = proj3.shape
    rows = l // GRID_W
    assert rows % NA_QROWS == 0 and rows >= 2 * WIN_R
    assert NA_QROWS * GRID_W == HG_CHUNK and NA_HEADS == 2 * HG_HEADS and NA_WIDTH == HG_HEADS * HG_DK
    n = rows // NA_QROWS
    blk = (None, HG_CHUNK, NA_WIDTH)

    def key_row0(g):
        return jnp.clip(NA_QROWS * g - WIN_R // 2, 0, rows - NA_KROWS)

    def variant(g):
        return jnp.where(g == 0, 0, jnp.where(g == n - 1, 2, 1))

    def spec(col, back=False):
        if back:
            return pl.BlockSpec(blk, lambda bi, g: (bi, n - 1 - g, col))
        return pl.BlockSpec(blk, lambda bi, g: (bi, g, col))

    kblk = (pl.Element(1), pl.Element(NA_KROWS * GRID_W), pl.Element(NA_WIDTH))
    return pl.pallas_call(
        _mixers_body,
        grid=(b, n),
        in_specs=[spec(COL_NA_Q),
                  pl.BlockSpec(kblk, lambda bi, g: (bi, key_row0(g) * GRID_W, COL_NA_K * NA_WIDTH)),
                  pl.BlockSpec(kblk, lambda bi, g: (bi, key_row0(g) * GRID_W, COL_NA_V * NA_WIDTH)),
                  pl.BlockSpec((None, NA_HEADS, NA_QROWS * GRID_W, NA_KROWS * GRID_W),
                               lambda bi, g: (variant(g), 0, 0, 0)),
                  spec(COL_HG_Q), spec(COL_HG_FF), spec(COL_HG_V),
                  spec(COL_HG_Q, True), spec(COL_HG_FB, True), spec(COL_HG_V, True),
                  pl.BlockSpec((2, HG_HEADS, 1, HG_DK), lambda bi, g: (0, 0, 0, 0))],
        out_specs=[spec(0), spec(0), spec(0, True)],
        out_shape=[jax.ShapeDtypeStruct((b, l, NA_WIDTH), BF16)] + [jax.ShapeDtypeStruct((b, l, NA_WIDTH), F32)] * 2,
        scratch_shapes=[pltpu.VMEM((2, HG_HEADS, HG_DV, HG_DK), F32)],
        compiler_params=_cparams(("parallel", "arbitrary"), 48),
        name="mixers_na_hgrn2",
    )(*([proj3] * 3), bias_tbl, *([proj3] * 6), lb4)


CONV_HALO = 16


def _conv_body(a_ref, b_ref, ap_ref, bp_ref, an_ref, bn_ref, w_ref, cb_ref, g_ref, be_ref, o_ref, u_ref, sh_ref):
    i = pl.program_id(1)
    n = pl.num_programs(1)
    t = a_ref.shape[0]

    def glu(a, b):
        return a[...].astype(F32) * _sigmoid(b[...].astype(F32))

    u_ref[pl.ds(0, CONV_HALO), :] = jnp.where(i > 0, glu(ap_ref, bp_ref), 0.0)
    u_ref[pl.ds(CONV_HALO, t), :] = glu(a_ref, b_ref)
    u_ref[pl.ds(CONV_HALO + t, CONV_HALO), :] = jnp.where(i < n - 1, glu(an_ref, bn_ref), 0.0)
    span = t + 2 * CONV_HALO - SUBLANES
    for r in range(1, SUBLANES):
        sh_ref[r - 1] = u_ref[pl.ds(r, span), :]
    base = CONV_HALO - CONV_K // 2
    acc = jnp.zeros((t, D_CONV), F32)
    for kk in range(CONV_K):
        off = base + kk
        r = off % SUBLANES
        win = u_ref[pl.ds(off, t), :] if r == 0 else sh_ref[r - 1, pl.ds(off - r, t), :]
        acc = acc + w_ref[kk:kk + 1, :] * win
    y = acc + cb_ref[...]
    mu = jnp.mean(y, axis=-1, keepdims=True)
    yc = y - mu
    var = jnp.mean(yc * yc, axis=-1, keepdims=True)
    yn = yc * lax.rsqrt(var + EPS) * g_ref[...] + be_ref[...]
    o_ref[...] = (yn * _sigmoid(yn)).astype(BF16)


def _conv(proj3, conv_w, conv_b, ln_g, ln_b, t=512):
    b, l, _ = proj3.shape
    nt = l // t
    hb = t // CONV_HALO
    nh = l // CONV_HALO
    main = lambda col: pl.BlockSpec((None, t, D_CONV), lambda bi, i: (bi, i, col))
    prev = lambda col: pl.BlockSpec((None, CONV_HALO, D_CONV), lambda bi, i: (bi, jnp.maximum(i * hb - 1, 0), col))
    nxt = lambda col: pl.BlockSpec((None, CONV_HALO, D_CONV), lambda bi, i: (bi, jnp.minimum((i + 1) * hb, nh - 1), col))
    vec = pl.BlockSpec((1, D_CONV), lambda bi, i: (0, 0))
    return pl.pallas_call(
        _conv_body,
        grid=(b, nt),
        in_specs=[main(COL_CV_A), main(COL_CV_B), prev(COL_CV_A), prev(COL_CV_B), nxt(COL_CV_A), nxt(COL_CV_B),
                  pl.BlockSpec((CONV_K, D_CONV), lambda bi, i: (0, 0)), vec, vec, vec],
        out_specs=pl.BlockSpec((None, t, D_CONV), lambda bi, i: (bi, i, 0)),
        out_shape=jax.ShapeDtypeStruct((b, l, D_CONV), BF16),
        scratch_shapes=[pltpu.VMEM((t + 2 * CONV_HALO, D_CONV), F32),
                        pltpu.VMEM((SUBLANES - 1, t + 2 * CONV_HALO - SUBLANES, D_CONV), F32)],
        compiler_params=_cparams(("parallel", "arbitrary"), 32),
        name="conv_module",
    )(*([proj3] * 6), conv_w, conv_b, ln_g, ln_b)


def _merge_body(na_ref, of_ref, ob_ref, hg_ref, cv_ref, gna_ref, ghg_ref, gcv_ref, x_ref,
                og_ref, wna_ref, whg_ref, wcv_ref, wout_ref, o_ref):
    o = of_ref[...] + ob_ref[...]
    parts = []
    for h in range(HG_HEADS):
        oh = o[:, h * HG_DV:(h + 1) * HG_DV]
        parts.append(oh * lax.rsqrt(jnp.mean(oh * oh, axis=-1, keepdims=True) + EPS))
    gg = hg_ref[...].astype(F32)
    hg = jnp.concatenate(parts, axis=1) * og_ref[...] * (gg * _sigmoid(gg))
    merged = (_sigmoid(gna_ref[...].astype(F32)) * _dot(na_ref[...], wna_ref[...])
              + _sigmoid(ghg_ref[...].astype(F32)) * _dot(hg.astype(BF16), whg_ref[...])
              + _sigmoid(gcv_ref[...].astype(F32)) * _dot(cv_ref[...], wcv_ref[...]))
    o_ref[...] = x_ref[...] + _dot(merged.astype(BF16), wout_ref[...])


def _merge(na2, of2, ob2, proj2, cv2, x2, og, wna, whg, wcv, wout, t=512):
    n = x2.shape[0]
    half = lambda col=0: pl.BlockSpec((t, 512), lambda i: (i, col))
    full = lambda col=0: pl.BlockSpec((t, D_MODEL), lambda i: (i, col))
    wspec = lambda r: pl.BlockSpec((r, D_MODEL), lambda i: (0, 0))
    return pl.pallas_call(
        _merge_body,
        grid=(n // t,),
        in_specs=[half(), half(), half(), half(COL_HG_G), half(), full(COL_G_NA), full(COL_G_HG), full(COL_G_CV),
                  full(), pl.BlockSpec((1, 512), lambda i: (0, 0)), wspec(512), wspec(512), wspec(512),
                  wspec(D_MODEL)],
        out_specs=full(),
        out_shape=jax.ShapeDtypeStruct((n, D_MODEL), F32),
        compiler_params=_cparams(("parallel",), 48),
        name="merge_out_proj",
    )(na2, of2, ob2, proj2, cv2, proj2, proj2, proj2, x2, og, wna, whg, wcv, wout)


def _memkv_body(m_ref, g_ref, w_ref, kn_ref, o_ref):
    kv = _dot(_rms_rows(m_ref[...], g_ref[...]).astype(BF16), w_ref[...])
    parts = []
    for h in range(MEM_HEADS):
        kh = kv[:, h * MEM_HD:(h + 1) * MEM_HD]
        parts.append(_rms_rows(kh, kn_ref[...]))
    parts.append(kv[:, MEM_WIDTH:])
    o_ref[...] = jnp.concatenate(parts, axis=1).astype(BF16)


def _memkv(mem2, g, wkv, kn, t=256):
    n = mem2.shape[0]
    return pl.pallas_call(
        _memkv_body,
        grid=(n // t,),
        in_specs=[pl.BlockSpec((t, D_MODEL), lambda i: (i, 0)), pl.BlockSpec((1, D_MODEL), lambda i: (0, 0)),
                  pl.BlockSpec((D_MODEL, 2 * MEM_WIDTH), lambda i: (0, 0)),
                  pl.BlockSpec((1, MEM_HD), lambda i: (0, 0))],
        out_specs=pl.BlockSpec((t, 2 * MEM_WIDTH), lambda i: (i, 0)),
        out_shape=jax.ShapeDtypeStruct((n, 2 * MEM_WIDTH), BF16),
        compiler_params=_cparams(("parallel",), 32),
        name="mem_kv",
    )(mem2, g, wkv, kn)


def _memattn_body(x_ref, g_ref, wq_ref, qn_ref, kv_ref, wo_ref, o_ref):
    x = x_ref[...]
    q = _dot(_rms_rows(x, g_ref[...]).astype(BF16), wq_ref[...])
    kv = kv_ref[...]
    outs = []
    for h in range(MEM_HEADS):
        sl = slice(h * MEM_HD, (h + 1) * MEM_HD)
        qh = _rms_rows(q[:, sl], qn_ref[...]) * (MEM_HD ** -0.5)
        s = _dot_nt(qh.astype(BF16), kv[:, sl])
        m = jnp.max(s, axis=-1, keepdims=True)
        p = jnp.exp(s - m)
        l = jnp.sum(p, axis=-1, keepdims=True)
        outs.append(_dot(p.astype(BF16), kv[:, MEM_WIDTH + h * MEM_HD:MEM_WIDTH + (h + 1) * MEM_HD]) / l)
    o_ref[...] = x + _dot(jnp.concatenate(outs, axis=1).astype(BF16), wo_ref[...])


def _memattn(x3, g, wq, qn, kv3, wo, t=512):
    b, l, _ = x3.shape
    m = kv3.shape[1]
    tok = pl.BlockSpec((None, t, D_MODEL), lambda bi, i: (bi, i, 0))
    const = lambda r, c: pl.BlockSpec((r, c), lambda bi, i: (0, 0))
    return pl.pallas_call(
        _memattn_body,
        grid=(b, l // t),
        in_specs=[tok, const(1, D_MODEL), const(D_MODEL, MEM_WIDTH), const(1, MEM_HD),
                  pl.BlockSpec((None, m, 2 * MEM_WIDTH), lambda bi, i: (bi, 0, 0)), const(MEM_WIDTH, D_MODEL)],
        out_specs=tok,
        out_shape=jax.ShapeDtypeStruct((b, l, D_MODEL), F32),
        compiler_params=_cparams(("parallel", "arbitrary"), 32),
        name="mem_attention",
    )(x3, g, wq, qn, kv3, wo)


def _router_body(x_ref, g_ref, w_ref, h_ref, a_ref):
    hn = _rms_rows(x_ref[...], g_ref[...])
    h_ref[...] = hn.astype(BF16)
    h_hi, h_lo = _split2(hn)
    w_hi, w_lo = _split2(w_ref[...])
    both = _dot(h_hi, jnp.concatenate([w_hi, w_lo], axis=1))
    logits = both[:, :N_EXPERTS] + both[:, N_EXPERTS:] + _dot(h_lo, w_hi)
    e = jnp.exp(logits - jnp.max(logits, axis=-1, keepdims=True))
    a_ref[...] = e / jnp.sum(e, axis=-1, keepdims=True)


def _router(x2, g, w_router, t=512):
    n = x2.shape[0]
    return pl.pallas_call(
        _router_body,
        grid=(n // t,),
        in_specs=[pl.BlockSpec((t, D_MODEL), lambda i: (i, 0)), pl.BlockSpec((1, D_MODEL), lambda i: (0, 0)),
                  pl.BlockSpec((D_MODEL, N_EXPERTS), lambda i: (0, 0))],
        out_specs=[pl.BlockSpec((t, D_MODEL), lambda i: (i, 0)), pl.BlockSpec((t, N_EXPERTS), lambda i: (i, 0))],
        out_shape=[jax.ShapeDtypeStruct((n, D_MODEL), BF16), jax.ShapeDtypeStruct((n, N_EXPERTS), F32)],
        compiler_params=_cparams(("parallel",), 32),
        name="router",
    )(x2, g, w_router)


def _thresh_body(a_ref, thr_ref, need_ref, *, cap):
    bits = pltpu.bitcast(a_ref[...], I32)

    def step(b, thr):
        cand = thr | (1 << (30 - b))
        cnt = jnp.sum((bits >= cand).astype(I32), axis=1, keepdims=True)
        return jnp.where(cnt >= cap, cand, thr)

    thr = lax.fori_loop(0, 31, step, jnp.zeros((N_EXPERTS, 1), I32))
    thr_ref[...] = thr
    need_ref[...] = cap - jnp.sum((bits > thr).astype(I32), axis=1, keepdims=True)


def _thresh(aff_t, cap):
    n = aff_t.shape[1]
    return pl.pallas_call(
        functools.partial(_thresh_body, cap=cap),
        grid=(1,),
        in_specs=[pl.BlockSpec((N_EXPERTS, n), lambda i: (0, 0))],
        out_specs=[pl.BlockSpec((N_EXPERTS, 1), lambda i: (0, 0))] * 2,
        out_shape=[jax.ShapeDtypeStruct((N_EXPERTS, 1), I32)] * 2,
        compiler_params=_cparams(("arbitrary",), 32),
        name="topk_threshold",
    )(aff_t)


def _slots_body(a_ref, thr_ref, need_ref, pos_ref, eq_off, sel_off):
    @pl.when(pl.program_id(0) == 0)
    def _():
        eq_off[...] = jnp.zeros_like(eq_off)
        sel_off[...] = jnp.zeros_like(sel_off)

    bits = pltpu.bitcast(a_ref[...], I32)
    w = bits.shape[1]
    thr = thr_ref[...]
    before = (lax.broadcasted_iota(I32, (w, w), 0) < lax.broadcasted_iota(I32, (w, w), 1)).astype(BF16)
    eq = bits == thr
    eq_f = eq.astype(F32)
    eq_rank = _dot(eq_f.astype(BF16), before) + eq_off[...]
    sel = (bits > thr) | (eq & (eq_rank < need_ref[...].astype(F32)))
    sel_f = sel.astype(F32)
    slot = _dot(sel_f.astype(BF16), before) + sel_off[...]
    pos_ref[...] = jnp.where(sel, slot, -1.0).astype(I32)
    eq_off[...] += jnp.sum(eq_f, axis=1, keepdims=True)
    sel_off[...] += jnp.sum(sel_f, axis=1, keepdims=True)


def _slots(aff_t, thr, need, w=TOK_BLK):
    n = aff_t.shape[1]
    col = pl.BlockSpec((N_EXPERTS, 1), lambda i: (0, 0))
    return pl.pallas_call(
        _slots_body,
        grid=(n // w,),
        in_specs=[pl.BlockSpec((N_EXPERTS, w), lambda i: (0, i)), col, col],
        out_specs=pl.BlockSpec((N_EXPERTS, w), lambda i: (0, i)),
        out_shape=jax.ShapeDtypeStruct((N_EXPERTS, n), I32),
        scratch_shapes=[pltpu.VMEM((N_EXPERTS, 1), F32)] * 2,
        compiler_params=_cparams(("arbitrary",), 32),
        name="topk_slots",
    )(aff_t, thr, need)


def _gather_body(cum_ref, nr_ref, h_ref, pos_ref, xe_hbm, stg, sem, *, cap):
    i = pl.program_id(0)
    nblk = pl.num_programs(0)
    win_rows = COMB_WIN * ROW_TILE

    def copies(blk, r, buf):
        return [pltpu.make_async_copy(
            stg.at[buf, pl.ds(e * win_rows, win_rows), :],
            xe_hbm.at[e, pl.ds(pl.multiple_of((cum_ref[e, blk] + r * COMB_WIN) * ROW_TILE, ROW_TILE), win_rows), :],
            sem.at[buf]) for e in range(N_EXPERTS)]

    w_iota = lax.broadcasted_iota(I32, (COMB_WIN, TOK_BLK), 0)

    def stage(r, buf):
        sel = []
        for e in range(N_EXPERTS):
            nominal = cum_ref[e, i] + r * COMB_WIN
            p = pos_ref[e:e + 1, :]
            sel.append((w_iota == jnp.where(p >= nominal, p - nominal, -1)).astype(BF16))
        res = _dot(jnp.concatenate(sel, axis=0), h_ref[...])
        for j in range(ROW_TILE):
            stg[buf, pl.ds(j, N_EXPERTS * COMB_WIN, stride=ROW_TILE), :] = res[:, j * LANES:(j + 1) * LANES]

    @pl.when(i == 0)
    def _():
        stg[2, pl.ds(0, win_rows), :] = jnp.zeros((win_rows, LANES), F32)
        pads = [pltpu.make_async_copy(stg.at[2, pl.ds(0, win_rows), :],
                                      xe_hbm.at[e, pl.ds(cap * ROW_TILE, win_rows), :], sem.at[2])
                for e in range(N_EXPERTS)]
        for c in pads:
            c.start()
        for c in pads:
            c.wait()

    slot = i % 2
    nr = nr_ref[i]
    stage(0, slot)

    @pl.when((i > 0) & (nr_ref[jnp.maximum(i - 1, 0)] == 1))
    def _():
        for c in copies(i - 1, 0, 1 - slot):
            c.wait()

    for c in copies(i, 0, slot):
        c.start()

    @pl.when(nr > 1)
    def _():
        for c in copies(i, 0, slot):
            c.wait()

        def extra(r, carry):
            stage(r, 2)
            for e, c in enumerate(copies(i, r, 2)):
                @pl.when(cum_ref[e, i + 1] > cum_ref[e, i] + r * COMB_WIN)
                def _():
                    c.start()
                    c.wait()
            return carry

        lax.fori_loop(1, nr, extra, 0)

    @pl.when((i == nblk - 1) & (nr == 1))
    def _():
        for c in copies(i, 0, slot):
            c.wait()


def _gather(cum, nrounds, hn, pos, cap):
    n = hn.shape[0]
    return pl.pallas_call(
        functools.partial(_gather_body, cap=cap),
        grid_spec=pltpu.PrefetchScalarGridSpec(
            num_scalar_prefetch=2,
            grid=(n // TOK_BLK,),
            in_specs=[pl.BlockSpec((TOK_BLK, D_MODEL), lambda i, cum, nr: (i, 0)),
                      pl.BlockSpec((N_EXPERTS, TOK_BLK), lambda i, cum, nr: (0, i))],
            out_specs=pl.BlockSpec(memory_space=pl.ANY),
            scratch_shapes=[pltpu.VMEM((3, N_EXPERTS * COMB_WIN * ROW_TILE, LANES), F32),
                            pltpu.SemaphoreType.DMA((3,))],
        ),
        out_shape=jax.ShapeDtypeStruct((N_EXPERTS, (cap + COMB_WIN) * ROW_TILE, LANES), F32),
        compiler_params=_cparams(("arbitrary",), 40),
        name="expert_gather",
    )(cum, nrounds, hn, pos)


def _ffn_body(x_ref, wg_ref, wu_ref, wd_ref, o_ref, acc_ref, xb_ref):
    f = pl.program_id(2)

    @pl.when(f == 0)
    def _():
        tm = xb_ref.shape[0]
        xb_ref[...] = jnp.concatenate([x_ref[pl.ds(j, tm, stride=ROW_TILE), :] for j in range(ROW_TILE)],
                                      axis=1).astype(BF16)

    x = xb_ref[...]
    gate = _dot(x, wg_ref[...])
    up = _dot(x, wu_ref[...])
    part = _dot((gate * _sigmoid(gate) * up).astype(BF16), wd_ref[...])

    last = pl.num_programs(2) - 1

    @pl.when(f == 0)
    def _():
        acc_ref[...] = part

    @pl.when((f != 0) & (f != last))
    def _():
        acc_ref[...] += part

    @pl.when(f == last)
    def _():
        total = acc_ref[...] + part
        tm = total.shape[0]
        for j in range(ROW_TILE):
            o_ref[pl.ds(j, tm, stride=ROW_TILE), :] = total[:, j * LANES:(j + 1) * LANES]


def _ffn(xe, wgu, wd, cap, tm=1024, tf=D_FF_PAD // 2):
    e = xe.shape[0]
    tm = min(tm, cap)
    nf = D_FF_PAD // tf
    assert nf >= 2
    return pl.pallas_call(
        _ffn_body,
        grid=(e, cap // tm, nf),
        in_specs=[pl.BlockSpec((None, tm * ROW_TILE, LANES), lambda ei, i, f: (ei, i, 0)),
                  pl.BlockSpec((None, D_MODEL, tf), lambda ei, i, f: (ei, 0, f)),
                  pl.BlockSpec((None, D_MODEL, tf), lambda ei, i, f: (ei, 0, f + nf)),
                  pl.BlockSpec((None, tf, D_MODEL), lambda ei, i, f: (ei, f, 0))],
        out_specs=pl.BlockSpec((None, tm * ROW_TILE, LANES), lambda ei, i, f: (ei, i, 0)),
        out_shape=jax.ShapeDtypeStruct((e, cap * ROW_TILE, LANES), F32),
        scratch_shapes=[pltpu.VMEM((tm, D_MODEL), F32), pltpu.VMEM((tm, D_MODEL), BF16)],
        compiler_params=_cparams(("parallel", "parallel", "arbitrary"), 56),
        name="expert_ffn",
    )(xe, wgu, wgu, wd)


def _combine_body(cum_ref, nr_ref, x_ref, aff_ref, pos_ref, y_hbm, o_ref, ybuf, sem, *, cap):
    i = pl.program_id(0)
    nblk = pl.num_programs(0)
    win_rows = COMB_WIN * ROW_TILE

    def win_start(blk, e, r):
        return jnp.minimum(cum_ref[e, blk] + r * COMB_WIN, cap - COMB_WIN)

    def copies(blk, r, buf):
        return [pltpu.make_async_copy(
            y_hbm.at[e, pl.ds(pl.multiple_of(win_start(blk, e, r) * ROW_TILE, ROW_TILE), win_rows), :],
            ybuf.at[buf, pl.ds(e * win_rows, win_rows), :],
            sem.at[buf]) for e in range(N_EXPERTS)]

    w_iota = lax.broadcasted_iota(I32, (COMB_WIN, TOK_BLK), 0)

    def contribution(r, buf):
        y = jnp.concatenate([ybuf[buf, pl.ds(j, N_EXPERTS * COMB_WIN, stride=ROW_TILE), :]
                             for j in range(ROW_TILE)], axis=1).astype(BF16)
        sel = []
        for e in range(N_EXPERTS):
            nominal = cum_ref[e, i] + r * COMB_WIN
            p = pos_ref[e:e + 1, :]
            d = jnp.where(p >= nominal, p - win_start(i, e, r), -1)
            sel.append(jnp.where(w_iota == d, aff_ref[e:e + 1, :], 0.0).astype(BF16))
        return _dot_tn(jnp.concatenate(sel, axis=0), y)

    slot = i % 2

    @pl.when(i == 0)
    def _():
        for c in copies(0, 0, 0):
            c.start()

    @pl.when(i + 1 < nblk)
    def _():
        for c in copies(i + 1, 0, 1 - slot):
            c.start()

    for c in copies(i, 0, slot):
        c.wait()
    o_ref[...] = x_ref[...] + contribution(0, slot)

    def extra(r, carry):
        for c in copies(i, r, 2):
            c.start()
        for c in copies(i, r, 2):
            c.wait()
        o_ref[...] += contribution(r, 2)
        return carry

    lax.fori_loop(1, nr_ref[i], extra, 0)


def _combine(cum, nrounds, x2, aff_t, pos, ye, cap):
    n = x2.shape[0]
    assert cap >= COMB_WIN
    tok = pl.BlockSpec((TOK_BLK, D_MODEL), lambda i, cum, nr: (i, 0))
    per_expert = pl.BlockSpec((N_EXPERTS, TOK_BLK), lambda i, cum, nr: (0, i))
    return pl.pallas_call(
        functools.partial(_combine_body, cap=cap),
        grid_spec=pltpu.PrefetchScalarGridSpec(
            num_scalar_prefetch=2,
            grid=(n // TOK_BLK,),
            in_specs=[tok, per_expert, per_expert, pl.BlockSpec(memory_space=pl.ANY)],
            out_specs=tok,
            scratch_shapes=[pltpu.VMEM((3, N_EXPERTS * COMB_WIN * ROW_TILE, LANES), F32),
                            pltpu.SemaphoreType.DMA((3,))],
        ),
        out_shape=jax.ShapeDtypeStruct((n, D_MODEL), F32),
        compiler_params=_cparams(("arbitrary",), 40),
        name="expert_combine",
    )(cum, nrounds, x2, aff_t, pos, ye)


def _expert_choice(x2, hn, aff, wgu, wd):
    n = x2.shape[0]
    cap = EC_CAPACITY * n // N_EXPERTS
    aff_t = aff.T
    thr, need = _thresh(aff_t, cap)
    pos = _slots(aff_t, thr, need)
    cnt = jnp.sum((pos >= 0).reshape(N_EXPERTS, n // TOK_BLK, TOK_BLK), axis=-1, dtype=I32)
    cum = jnp.concatenate([jnp.zeros((N_EXPERTS, 1), I32), jnp.cumsum(cnt, axis=1, dtype=I32)], axis=1)
    nrounds = jnp.maximum(1, (jnp.max(cnt, axis=0) + COMB_WIN - 1) // COMB_WIN).astype(I32)
    xe = _gather(cum, nrounds, hn, pos, cap)
    ye = _ffn(xe, wgu, wd, cap)
    return _combine(cum, nrounds, x2, aff_t, pos, ye, cap)


def _pad_gate_up_body(x_ref, o_ref):
    x = x_ref[...]
    zeros = jnp.zeros((x.shape[0], D_FF_PAD - D_FF), BF16)
    o_ref[:, 0:D_FF] = x[:, :D_FF].astype(BF16)
    o_ref[:, D_FF:D_FF_PAD] = zeros
    o_ref[:, D_FF_PAD:D_FF_PAD + D_FF] = x[:, D_FF:].astype(BF16)
    o_ref[:, D_FF_PAD + D_FF:] = zeros


def _pad_gate_up(w, layer, t=256):
    e = w.shape[1]
    return pl.pallas_call(
        _pad_gate_up_body,
        grid=(e, D_MODEL // t),
        in_specs=[pl.BlockSpec((None, None, t, 2 * D_FF), lambda ei, i: (layer, ei, i, 0))],
        out_specs=pl.BlockSpec((None, t, 2 * D_FF_PAD), lambda ei, i: (ei, i, 0)),
        out_shape=jax.ShapeDtypeStruct((e, D_MODEL, 2 * D_FF_PAD), BF16),
        compiler_params=_cparams(("parallel", "parallel"), 32),
        name="pad_gate_up",
    )(w)


def _pad_down_body(x_ref, o_ref):
    x = x_ref[...]
    row = lax.broadcasted_iota(I32, x.shape, 0) + pl.program_id(1) * x.shape[0]
    o_ref[...] = jnp.where(row < D_FF, x, 0.0).astype(BF16)


def _pad_down(w, layer, t=D_FF_PAD // 4):
    e = w.shape[1]
    return pl.pallas_call(
        _pad_down_body,
        grid=(e, D_FF_PAD // t),
        in_specs=[pl.BlockSpec((None, None, t, D_MODEL), lambda ei, i: (layer, ei, i, 0))],
        out_specs=pl.BlockSpec((None, t, D_MODEL), lambda ei, i: (ei, i, 0)),
        out_shape=jax.ShapeDtypeStruct((e, D_FF_PAD, D_MODEL), BF16),
        compiler_params=_cparams(("parallel", "parallel"), 32),
        name="pad_down",
    )(w)


def _prep_layer(l, p, lb):
    row = lambda a: a.astype(F32).reshape(1, -1)
    qkg = jnp.concatenate([jnp.tile(p["na_q_norm"][l] * (NA_HD ** -0.5), NA_HEADS),
                           jnp.tile(p["na_k_norm"][l], NA_HEADS)]).reshape(1, 2 * NA_WIDTH)
    idx = jnp.arange(MXU_DIM) // NA_HD
    return dict(
        norm_mix=row(p["norm_mix"][l]), w_in=p["w_in"][l].astype(BF16),
        grp=(idx[:, None] == idx[None, :]).astype(BF16), qkg=qkg.astype(F32),
        bias_tbl=_na_bias_table(p["na_rel_bias"][l]),
        lb4=lb[:, l].reshape(2, HG_HEADS, 1, HG_DK),
        conv_w=p["conv_w"][l].astype(F32), conv_b=row(p["conv_b"][l]),
        ln_g=row(p["conv_ln_g"][l]), ln_b=row(p["conv_ln_b"][l]),
        og=jnp.tile(p["hg_out_norm"][l], HG_HEADS).reshape(1, -1).astype(F32),
        wna=p["w_na_br"][l].astype(BF16), whg=p["w_hg_br"][l].astype(BF16), wcv=p["w_cv_br"][l].astype(BF16),
        wout=p["w_out"][l].astype(BF16),
        norm_mem=row(p["norm_mem"][l]), mem_kv_norm=row(p["mem_kv_norm"][l]),
        wq=p["wq_mem"][l].astype(BF16), wkv=p["wkv_mem"][l].astype(BF16),
        qn=row(p["mem_q_norm"][l]), kn=row(p["mem_k_norm"][l]), wo=p["wo_mem"][l].astype(BF16),
        norm_ffn=row(p["norm_ffn"][l]), w_router=p["w_router"][l].astype(F32),
        wgu=_pad_gate_up(p["w_gate_up"], l), wd=_pad_down(p["w_down"], l),
    )


def _layer(x3, mem3, w):
    b, l, _ = x3.shape
    n = b * l
    x2 = x3.reshape(n, D_MODEL)
    proj2 = _inproj(x2, w["norm_mix"], w["w_in"], w["grp"], w["qkg"])
    proj3 = proj2.reshape(b, l, D_IN)
    na, o_f, o_b = _mixers(proj3, w["bias_tbl"], w["lb4"])
    cv = _conv(proj3, w["conv_w"], w["conv_b"], w["ln_g"], w["ln_b"])
    x2 = _merge(na.reshape(n, -1), o_f.reshape(n, -1), o_b.reshape(n, -1), proj2, cv.reshape(n, -1), x2,
                w["og"], w["wna"], w["whg"], w["wcv"], w["wout"])
    kv = _memkv(mem3.reshape(-1, D_MODEL), w["mem_kv_norm"], w["wkv"], w["kn"])
    x3 = _memattn(x2.reshape(b, l, D_MODEL), w["norm_mem"], w["wq"], w["qn"],
                  kv.reshape(b, -1, 2 * MEM_WIDTH), w["wo"])
    x2 = x3.reshape(n, D_MODEL)
    hn, aff = _router(x2, w["norm_ffn"], w["w_router"])
    x2 = _expert_choice(x2, hn, aff, w["wgu"], w["wd"])
    return x2.reshape(b, l, D_MODEL)


def kernel(x_prompt, x_sample, mem_prompt, mem_sample, norm_mix, w_in, na_q_norm, na_k_norm, na_rel_bias, w_na_br, hg_lb, hg_out_norm, w_hg_br, conv_w, conv_b, conv_ln_g, conv_ln_b, w_cv_br, w_out, norm_mem, mem_kv_norm, wq_mem, wkv_mem, mem_q_norm, mem_k_norm, wo_mem, norm_ffn, w_router, w_gate_up, w_down):
    p = dict(norm_mix=norm_mix, w_in=w_in, na_q_norm=na_q_norm, na_k_norm=na_k_norm, na_rel_bias=na_rel_bias,
             w_na_br=w_na_br, hg_out_norm=hg_out_norm, w_hg_br=w_hg_br, conv_w=conv_w, conv_b=conv_b,
             conv_ln_g=conv_ln_g, conv_ln_b=conv_ln_b, w_cv_br=w_cv_br, w_out=w_out, norm_mem=norm_mem,
             mem_kv_norm=mem_kv_norm, wq_mem=wq_mem, wkv_mem=wkv_mem, mem_q_norm=mem_q_norm,
             mem_k_norm=mem_k_norm, wo_mem=wo_mem, norm_ffn=norm_ffn, w_router=w_router, w_gate_up=w_gate_up,
             w_down=w_down)
    depth = w_in.shape[0]
    sm = jax.nn.softmax(hg_lb.astype(F32), axis=1)
    lb = jnp.cumsum(sm, axis=1) - sm[:, :1]
    xs = [x_prompt, x_sample]
    mems = [mem_prompt, mem_sample]
    for l in range(depth):
        w = _prep_layer(l, p, lb)
        xs = [_layer(x, m, w) for x, m in zip(xs, mems)]
    return tuple(xs)
```
